```python
import jax, jax.numpy as jnp
from jax import lax
import numpy as np

D_MODEL = 1024
BATCH = 8
SEQ = 2048
DEPTH = 1

GRID_W = 64
CTX_LEN = 256
N_HEADS = 16
N_KV_HEADS = 4
GROUP = N_HEADS // N_KV_HEADS
HEAD_DIM = 64
ATTN_WIDTH = N_HEADS * HEAD_DIM
KV_WIDTH = N_KV_HEADS * HEAD_DIM
WINDOW = 128
BLOCK = 128
ROPE_AXIS_DIM = HEAD_DIM // 2
ROPE_BASE = 10000.0
CONV_CH = D_MODEL // 2
CONV_WIDTH = 31
N_BRANCHES = 2
Q_OFF = 0
K_OFF = Q_OFF + ATTN_WIDTH
V_OFF = K_OFF + KV_WIDTH
CONV_OFF = V_OFF + KV_WIDTH
GATE_OFF = CONV_OFF + 2 * CONV_CH
IN_WIDTH = GATE_OFF + N_BRANCHES * D_MODEL
PEER_HEADS = 8
N_KEYS = 128
N_EXPERTS = N_KEYS * N_KEYS
PEER_QDIM = 256
PEER_HALF = PEER_QDIM // 2
PEER_TOPK = 16
PEER_CHUNK = 128
NORM_EPS = 1e-6
NEG_INF = -1e30

kernel_name = "hybrid_dit_wgqa_conformer_peer"


def rmsnorm(x, g):
    xf = x.astype(jnp.float32)
    y = xf * lax.rsqrt(jnp.mean(xf * xf, axis=-1, keepdims=True) + NORM_EPS)
    return (y * g.astype(jnp.float32)).astype(x.dtype)


def layernorm(x, g, b):
    xf = x.astype(jnp.float32)
    mu = jnp.mean(xf, axis=-1, keepdims=True)
    var = jnp.mean(jnp.square(xf - mu), axis=-1, keepdims=True)
    y = (xf - mu) * lax.rsqrt(var + NORM_EPS)
    return (y * g.astype(jnp.float32) + b.astype(jnp.float32)).astype(x.dtype)


def adaln(cond, w_ada, b_ada):
    return jnp.split(jax.nn.silu(cond) @ w_ada + b_ada, 6, axis=-1)


def modulate(h, shift, scale):
    return h * (1.0 + scale) + shift


def axial_rope_tables(rows):
    t = jnp.arange(rows * GRID_W, dtype=jnp.int32)
    row = (t // GRID_W).astype(jnp.float32)
    col = (t % GRID_W).astype(jnp.float32)
    freqs = ROPE_BASE ** (-jnp.arange(0, ROPE_AXIS_DIM, 2, dtype=jnp.float32) / ROPE_AXIS_DIM)
    ang_r = row[:, None] * freqs[None, :]
    ang_c = col[:, None] * freqs[None, :]
    return (jnp.cos(ang_r), jnp.sin(ang_r), jnp.cos(ang_c), jnp.sin(ang_c))


def rope_rotate_half(x, cos, sin):
    x1, x2 = jnp.split(x, 2, axis=-1)
    cos = cos[None, :, None, :]
    sin = sin[None, :, None, :]
    return jnp.concatenate([x1 * cos - x2 * sin, x2 * cos + x1 * sin], axis=-1)


def apply_axial_rope(x, rope):
    cos_r, sin_r, cos_c, sin_c = rope
    xf = x.astype(jnp.float32)
    xr = rope_rotate_half(xf[..., :ROPE_AXIS_DIM], cos_r, sin_r)
    xc = rope_rotate_half(xf[..., ROPE_AXIS_DIM:], cos_c, sin_c)
    return jnp.concatenate([xr, xc], axis=-1).astype(x.dtype)


def window_attention(q, k, v, k_ctx, v_ctx, sink):
    B, S = q.shape[:2]
    nb = S // BLOCK
    qb = q.reshape(B, nb, BLOCK, N_KV_HEADS, GROUP, HEAD_DIM)
    pad = ((0, 0), (WINDOW, WINDOW), (0, 0), (0, 0))
    kb = jnp.pad(k, pad).reshape(B, nb + 2, BLOCK, N_KV_HEADS, HEAD_DIM)
    vb = jnp.pad(v, pad).reshape(B, nb + 2, BLOCK, N_KV_HEADS, HEAD_DIM)
    k_win = jnp.concatenate([kb[:, :-2], kb[:, 1:-1], kb[:, 2:]], axis=2)
    v_win = jnp.concatenate([vb[:, :-2], vb[:, 1:-1], vb[:, 2:]], axis=2)
    qpos = jnp.arange(nb)[:, None, None] * BLOCK + jnp.arange(BLOCK)[None, :, None]
    kpos = (jnp.arange(nb)[:, None, None] - 1) * BLOCK + jnp.arange(3 * BLOCK)[None, None, :]
    valid = (kpos >= 0) & (kpos < S) & (jnp.abs(qpos - kpos) <= WINDOW)
    s_win = jnp.einsum('bnqhgd,bnkhd->bnhgqk', qb, k_win).astype(jnp.float32)
    s_win = jnp.where(valid[None, :, None, None], s_win, NEG_INF)
    s_ctx = jnp.einsum('bnqhgd,bchd->bnhgqc', qb, k_ctx).astype(jnp.float32)
    s_sink = jnp.broadcast_to(
        sink.astype(jnp.float32).reshape(N_KV_HEADS, GROUP)[None, None, :, :, None, None],
        s_win.shape[:-1] + (1,))
    p = jax.nn.softmax(jnp.concatenate([s_win, s_ctx, s_sink], axis=-1), axis=-1)
    n_win = 3 * BLOCK
    n_ctx = k_ctx.shape[1]
    p_win = p[..., :n_win].astype(v.dtype)
    p_ctx = p[..., n_win:n_win + n_ctx].astype(v.dtype)
    o = (jnp.einsum('bnhgqk,bnkhd->bnqhgd', p_win, v_win)
         + jnp.einsum('bnhgqc,bchd->bnqhgd', p_ctx, v_ctx))
    return o.reshape(B, S, ATTN_WIDTH)


def context_attention(q, k, v, sink):
    B, L = q.shape[:2]
    qg = q.reshape(B, L, N_KV_HEADS, GROUP, HEAD_DIM)
    s = jnp.einsum('blhgd,bmhd->bhglm', qg, k).astype(jnp.float32)
    s_sink = jnp.broadcast_to(
        sink.astype(jnp.float32).reshape(N_KV_HEADS, GROUP)[None, :, :, None, None],
        s.shape[:-1] + (1,))
    p = jax.nn.softmax(jnp.concatenate([s, s_sink], axis=-1), axis=-1)[..., :L]
    o = jnp.einsum('bhglm,bmhd->blhgd', p.astype(v.dtype), v)
    return o.reshape(B, L, ATTN_WIDTH)


def conformer_conv(u_glu, conv_dw, conv_dw_b, conv_ln_g, conv_ln_b, w_o_conv):
    a, g = jnp.split(u_glu, 2, axis=-1)
    u = a * jax.nn.sigmoid(g)
    u = lax.conv_general_dilated(
        u, conv_dw[:, None, :].astype(u.dtype), window_strides=(1,),
        padding=[(CONV_WIDTH // 2, CONV_WIDTH // 2)],
        dimension_numbers=('NWC', 'WIO', 'NWC'), feature_group_count=CONV_CH) + conv_dw_b
    u = jax.nn.silu(layernorm(u, conv_ln_g, conv_ln_b))
    return u @ w_o_conv


def merge_branches(y_attn, y_conv, gate_logits, w_out):
    g_attn, g_conv = jnp.split(gate_logits, N_BRANCHES, axis=-1)
    return (jax.nn.sigmoid(g_attn) * y_attn + jax.nn.sigmoid(g_conv) * y_conv) @ w_out


def peer_ffn(h, peer_wq, peer_keys, peer_u, peer_v):
    B, S, D = h.shape
    t = B * S
    hf = h.reshape(t, D)
    q = (hf @ peer_wq).reshape(t, PEER_HEADS, 2, PEER_HALF)
    s = jnp.einsum('thpd,hpkd->thpk', q, peer_keys).astype(jnp.float32)
    s1, i1 = lax.top_k(s[:, :, 0], PEER_TOPK)
    s2, i2 = lax.top_k(s[:, :, 1], PEER_TOPK)
    cand = (s1[..., :, None] + s2[..., None, :]).reshape(t, PEER_HEADS, PEER_TOPK * PEER_TOPK)
    best, pos = lax.top_k(cand, PEER_TOPK)
    idx = (jnp.take_along_axis(i1, pos // PEER_TOPK, axis=-1) * N_KEYS
           + jnp.take_along_axis(i2, pos % PEER_TOPK, axis=-1))
    gates = jax.nn.softmax(best, axis=-1).astype(h.dtype)
    n_chunks = t // PEER_CHUNK

    def expert_block(args):
        xb, ib, gb = args
        act = jax.nn.gelu(jnp.einsum('td,thkd->thk', xb, peer_u[ib]), approximate=False)
        return jnp.einsum('thk,thkd->td', gb * act, peer_v[ib])

    out = lax.map(expert_block, (hf.reshape(n_chunks, PEER_CHUNK, D),
                                 idx.reshape(n_chunks, PEER_CHUNK, PEER_HEADS, PEER_TOPK),
                                 gates.reshape(n_chunks, PEER_CHUNK, PEER_HEADS, PEER_TOPK)))
    return out.reshape(B, S, D)


def trunk_layer(x, h_ctx, c, c_ctx, rope, update_ctx,
                w_ada, b_ada, g_pre_mix, g_post_mix, g_pre_ffn, g_post_ffn,
                w_in, attn_sink, w_o_attn, conv_dw, conv_dw_b, conv_ln_g, conv_ln_b,
                w_o_conv, w_out, peer_wq, peer_keys, peer_u, peer_v):
    B, S, _ = x.shape
    L = h_ctx.shape[1]
    sh1, sc1, gt1, sh2, sc2, gt2 = [m[:, None, :] for m in adaln(c, w_ada, b_ada)]
    csh1, csc1, cgt1, csh2, csc2, cgt2 = adaln(c_ctx, w_ada, b_ada)
    q_scale = HEAD_DIM ** -0.5

    hx = modulate(rmsnorm(x, g_pre_mix), sh1, sc1)
    hc = modulate(rmsnorm(h_ctx, g_pre_mix), csh1, csc1)
    p = hx @ w_in
    q = apply_axial_rope(p[..., Q_OFF:K_OFF].reshape(B, S, N_HEADS, HEAD_DIM), rope) * q_scale
    k = apply_axial_rope(p[..., K_OFF:V_OFF].reshape(B, S, N_KV_HEADS, HEAD_DIM), rope)
    v = p[..., V_OFF:CONV_OFF].reshape(B, S, N_KV_HEADS, HEAD_DIM)
    if update_ctx:
        pc = hc @ w_in
        kvc = pc[..., K_OFF:CONV_OFF]
    else:
        kvc = hc @ w_in[:, K_OFF:CONV_OFF]
    k_c = kvc[..., :KV_WIDTH].reshape(B, L, N_KV_HEADS, HEAD_DIM)
    v_c = kvc[..., KV_WIDTH:].reshape(B, L, N_KV_HEADS, HEAD_DIM)

    y_attn = window_attention(q, k, v, k_c, v_c, attn_sink) @ w_o_attn
    y_conv = conformer_conv(p[..., CONV_OFF:GATE_OFF], conv_dw, conv_dw_b, conv_ln_g, conv_ln_b, w_o_conv)
    mixed = merge_branches(y_attn, y_conv, p[..., GATE_OFF:], w_out)
    x = x + gt1 * rmsnorm(mixed, g_post_mix)

    if update_ctx:
        q_c = pc[..., Q_OFF:K_OFF].reshape(B, L, N_HEADS, HEAD_DIM) * q_scale
        yc_attn = context_attention(q_c, k_c, v_c, attn_sink) @ w_o_attn
        yc_conv = conformer_conv(pc[..., CONV_OFF:GATE_OFF], conv_dw, conv_dw_b, conv_ln_g, conv_ln_b, w_o_conv)
        mixed_c = merge_branches(yc_attn, yc_conv, pc[..., GATE_OFF:], w_out)
        h_ctx = h_ctx + cgt1 * rmsnorm(mixed_c, g_post_mix)

    h2 = modulate(rmsnorm(x, g_pre_ffn), sh2, sc2)
    x = x + gt2 * rmsnorm(peer_ffn(h2, peer_wq, peer_keys, peer_u, peer_v), g_post_ffn)
    if update_ctx:
        hc2 = modulate(rmsnorm(h_ctx, g_pre_ffn), csh2, csc2)
        h_ctx = h_ctx + cgt2 * rmsnorm(peer_ffn(hc2, peer_wq, peer_keys, peer_u, peer_v), g_post_ffn)
    return x, h_ctx


def setup_inputs(seed: int = 0) -> dict:
    key = jax.random.key(seed)
    ks = jax.random.split(key, 24)
    f32 = jnp.float32
    D = D_MODEL

    def nrm(k, shape, scale):
        return jax.random.normal(k, shape, f32) * scale

    def gain(k, shape):
        return 1.0 + 0.05 * jax.random.normal(k, shape, f32)

    return {
        "x": nrm(ks[0], (BATCH, SEQ, D), 1.0),
        "c": nrm(ks[1], (BATCH, D), 1.0),
        "ctx": nrm(ks[2], (BATCH, CTX_LEN, D), 1.0),
        "c_ctx": nrm(ks[3], (D,), 1.0),
        "w_ada": nrm(ks[4], (DEPTH, D, 6 * D), 0.5 * D ** -0.5),
        "b_ada": nrm(ks[5], (DEPTH, 6 * D), 0.01),
        "g_pre_mix": gain(ks[6], (DEPTH, D)),
        "g_post_mix": gain(ks[7], (DEPTH, D)),
        "g_pre_ffn": gain(ks[8], (DEPTH, D)),
        "g_post_ffn": gain(ks[9], (DEPTH, D)),
        "w_in": nrm(ks[10], (DEPTH, D, IN_WIDTH), D ** -0.5),
        "attn_sink": nrm(ks[11], (DEPTH, N_HEADS), 0.5),
        "w_o_attn": nrm(ks[12], (DEPTH, ATTN_WIDTH, D), ATTN_WIDTH ** -0.5),
        "conv_dw": nrm(ks[13], (DEPTH, CONV_WIDTH, CONV_CH), CONV_WIDTH ** -0.5),
        "conv_dw_b": nrm(ks[14], (DEPTH, CONV_CH), 0.01),
        "conv_ln_g": gain(ks[15], (DEPTH, CONV_CH)),
        "conv_ln_b": nrm(ks[16], (DEPTH, CONV_CH), 0.01),
        "w_o_conv": nrm(ks[17], (DEPTH, CONV_CH, D), CONV_CH ** -0.5),
        "w_out": nrm(ks[18], (DEPTH, D, D), D ** -0.5),
        "peer_wq": nrm(ks[19], (DEPTH, D, PEER_HEADS * PEER_QDIM), D ** -0.5),
        "peer_keys": nrm(ks[20], (DEPTH, PEER_HEADS, 2, N_KEYS, PEER_HALF), PEER_HALF ** -0.5),
        "peer_u": nrm(ks[21], (DEPTH, N_EXPERTS, D), D ** -0.5),
        "peer_v": nrm(ks[22], (DEPTH, N_EXPERTS, D), D ** -0.5),
    }


def reference(x, c, ctx, c_ctx, w_ada, b_ada, g_pre_mix, g_post_mix, g_pre_ffn, g_post_ffn,
              w_in, attn_sink, w_o_attn, conv_dw, conv_dw_b, conv_ln_g, conv_ln_b, w_o_conv,
              w_out, peer_wq, peer_keys, peer_u, peer_v):
    rows = x.shape[1] // GRID_W
    rope = axial_rope_tables(rows)
    h_ctx = ctx
    for layer in range(DEPTH):
        x, h_ctx = trunk_layer(
            x, h_ctx, c, c_ctx, rope, layer + 1 < DEPTH,
            w_ada[layer], b_ada[layer], g_pre_mix[layer], g_post_mix[layer],
            g_pre_ffn[layer], g_post_ffn[layer], w_in[layer], attn_sink[layer],
            w_o_attn[layer], conv_dw[layer], conv_dw_b[layer], conv_ln_g[layer],
            conv_ln_b[layer], w_o_conv[layer], w_out[layer], peer_wq[layer],
            peer_keys[layer], peer_u[layer], peer_v[layer])
    return x
```

```python
import functools

import jax
import jax.numpy as jnp
from jax import lax
from jax.experimental import pallas as pl
from jax.experimental.pallas import tpu as pltpu

F32 = jnp.float32
BF16 = jnp.bfloat16

D_MODEL = 1024
GRID_W = 64
N_HEADS = 16
N_KV_HEADS = 4
GROUP = N_HEADS // N_KV_HEADS
HEAD_DIM = 64
ATTN_WIDTH = N_HEADS * HEAD_DIM
KV_WIDTH = N_KV_HEADS * HEAD_DIM
WINDOW = 128
BLOCK = 128
ROPE_AXIS_DIM = HEAD_DIM // 2
ROPE_BASE = 10000.0
CONV_CH = D_MODEL // 2
CONV_WIDTH = 31
CONV_PAD = CONV_WIDTH // 2
K_OFF = ATTN_WIDTH
V_OFF = K_OFF + KV_WIDTH
CONV_OFF = V_OFF + KV_WIDTH
GATE_OFF = CONV_OFF + 2 * CONV_CH
IN_WIDTH = GATE_OFF + 2 * D_MODEL
PEER_HEADS = 8
N_KEYS = 128
N_EXPERTS = N_KEYS * N_KEYS
PEER_QDIM = 256
PEER_HALF = PEER_QDIM // 2
PEER_TOPK = 16
NORM_EPS = 1e-6
NEG_INF = -1e30
NEG_BIG = -3.0e38
SQRT_HALF = 0.7071067811865476
Q_SCALE = HEAD_DIM ** -0.5

LANES = 128
HALO = 16
VMEM_LIMIT = 56 * 1024 * 1024

_NT = (((1,), (1,)), ((), ()))


def _params(n_axes, vmem=VMEM_LIMIT):
    return pltpu.CompilerParams(dimension_semantics=("arbitrary",) * n_axes, vmem_limit_bytes=vmem)


def _rms(x, g):
    return x * lax.rsqrt(jnp.mean(x * x, axis=-1, keepdims=True) + NORM_EPS) * g


def _sigmoid(x):
    return jax.nn.sigmoid(x)


def _ada_kernel(c_ref, w_ref, b_ref, o_ref):
    c = c_ref[...]
    s = (c * _sigmoid(c)).astype(BF16)
    o_ref[...] = jnp.dot(s, w_ref[...].astype(BF16), preferred_element_type=F32) + b_ref[...]


def _ada(cond, w_ada, b_ada):
    rows, d = cond.shape
    n = w_ada.shape[1]
    bn = 1536
    return pl.pallas_call(
        _ada_kernel,
        grid=(n // bn,),
        in_specs=[pl.BlockSpec((rows, d), lambda j: (0, 0)),
                  pl.BlockSpec((d, bn), lambda j: (0, j)),
                  pl.BlockSpec((1, bn), lambda j: (0, j))],
        out_specs=pl.BlockSpec((rows, bn), lambda j: (0, j)),
        out_shape=jax.ShapeDtypeStruct((rows, n), F32),
        compiler_params=_params(1),
        name="ada",
    )(cond, w_ada, b_ada.reshape(1, n))


def _in_kernel(x_ref, sh_ref, sc_ref, g_ref, tab_ref, w_ref, q_ref, k_ref, v_ref, u_ref, gate_ref):
    h = (_rms(x_ref[...], g_ref[...]) * (1.0 + sc_ref[0]) + sh_ref[0]).astype(BF16)
    cos = tab_ref[:, 0:LANES]
    sin_lo = tab_ref[:, LANES:2 * LANES]
    sin_hi = tab_ref[:, 2 * LANES:3 * LANES]

    def proj(lo, n):
        return jnp.dot(h, w_ref[:, lo:lo + n], preferred_element_type=F32)

    def rope(p):
        return (p * cos + pltpu.roll(p, LANES - ROPE_AXIS_DIM // 2, 1) * sin_lo
                + pltpu.roll(p, ROPE_AXIS_DIM // 2, 1) * sin_hi)

    def rope2(p):
        return jnp.concatenate([rope(p[:, :LANES]), rope(p[:, LANES:])], axis=1)

    for g in range(GROUP):
        q_ref[g] = (rope2(proj(g * KV_WIDTH, KV_WIDTH)) * Q_SCALE).astype(BF16)
    k_ref[...] = rope2(proj(K_OFF, KV_WIDTH)).astype(BF16)
    v_ref[...] = proj(V_OFF, KV_WIDTH).astype(BF16)
    u_ref[...] = proj(CONV_OFF, CONV_CH) * _sigmoid(proj(CONV_OFF + CONV_CH, CONV_CH))
    for j in range(4):
        gate_ref[:, j * 512:(j + 1) * 512] = _sigmoid(proj(GATE_OFF + j * 512, 512)).astype(BF16)


def _in_proj(x2, sh, sc, g, tab, w, seq, tm):
    t, d = x2.shape
    return pl.pallas_call(
        _in_kernel,
        grid=(t // tm,),
        in_specs=[pl.BlockSpec((tm, d), lambda i: (i, 0)),
                  pl.BlockSpec((1, 1, d), lambda i: ((i * tm) // seq, 0, 0)),
                  pl.BlockSpec((1, 1, d), lambda i: ((i * tm) // seq, 0, 0)),
                  pl.BlockSpec((1, d), lambda i: (0, 0)),
                  pl.BlockSpec((tm, 3 * LANES), lambda i: (i % (seq // tm), 0)),
                  pl.BlockSpec((d, IN_WIDTH), lambda i: (0, 0))],
        out_specs=[pl.BlockSpec((GROUP, tm, KV_WIDTH), lambda i: (0, i, 0)),
                   pl.BlockSpec((tm, KV_WIDTH), lambda i: (i, 0)),
                   pl.BlockSpec((tm, KV_WIDTH), lambda i: (i, 0)),
                   pl.BlockSpec((tm, CONV_CH), lambda i: (i, 0)),
                   pl.BlockSpec((tm, 2 * d), lambda i: (i, 0))],
        out_shape=[jax.ShapeDtypeStruct((GROUP, t, KV_WIDTH), BF16),
                   jax.ShapeDtypeStruct((t, KV_WIDTH), BF16),
                   jax.ShapeDtypeStruct((t, KV_WIDTH), BF16),
                   jax.ShapeDtypeStruct((t, CONV_CH), F32),
                   jax.ShapeDtypeStruct((t, 2 * d), BF16)],
        compiler_params=_params(1),
        name="in_proj",
    )(x2, sh, sc, g, tab, w)


def _ctx_kernel(x_ref, sh_ref, sc_ref, g_ref, w_ref, k_ref, v_ref):
    h = (_rms(x_ref[...], g_ref[...]) * (1.0 + sc_ref[...]) + sh_ref[...]).astype(BF16)
    k_ref[...] = jnp.dot(h, w_ref[:, :KV_WIDTH], preferred_element_type=F32).astype(BF16)
    v_ref[...] = jnp.dot(h, w_ref[:, KV_WIDTH:], preferred_element_type=F32).astype(BF16)


def _ctx_proj(c2, sh, sc, g, w, tm):
    t, d = c2.shape
    return pl.pallas_call(
        _ctx_kernel,
        grid=(t // tm,),
        in_specs=[pl.BlockSpec((tm, d), lambda i: (i, 0)),
                  pl.BlockSpec((1, d), lambda i: (0, 0)),
                  pl.BlockSpec((1, d), lambda i: (0, 0)),
                  pl.BlockSpec((1, d), lambda i: (0, 0)),
                  pl.BlockSpec((d, 2 * KV_WIDTH), lambda i: (0, 0))],
        out_specs=[pl.BlockSpec((tm, KV_WIDTH), lambda i: (i, 0)),
                   pl.BlockSpec((tm, KV_WIDTH), lambda i: (i, 0))],
        out_shape=[jax.ShapeDtypeStruct((t, KV_WIDTH), BF16),
                   jax.ShapeDtypeStruct((t, KV_WIDTH), BF16)],
        compiler_params=_params(1),
        name="ctx_proj",
    )(c2, sh, sc, g, w)


def _attn_kernel(sink_ref, q_ref, kp_ref, kc_ref, kn_ref, vp_ref, vc_ref, vn_ref, kx_ref, vx_ref,
                 o_ref, *, nb):
    n = pl.program_id(1)
    qs = q_ref[...].reshape(GROUP * BLOCK, KV_WIDTH)
    kw = jnp.concatenate([kp_ref[...], kc_ref[...], kn_ref[...], kx_ref[...]], axis=0)
    vw = jnp.concatenate([vp_ref[...], vc_ref[...], vn_ref[...], vx_ref[...]], axis=0)
    nk = kw.shape[0]
    qi = lax.broadcasted_iota(jnp.int32, (BLOCK, nk), 0)
    kk = lax.broadcasted_iota(jnp.int32, (BLOCK, nk), 1)
    lo = jnp.where(n == 0, BLOCK, 0)
    hi = jnp.where(n == nb - 1, 2 * BLOCK, 3 * BLOCK)
    in_win = (kk >= qi) & (kk <= qi + 2 * WINDOW) & (kk >= lo) & (kk < hi)
    bias1 = jnp.where(in_win | (kk >= 3 * BLOCK), 0.0, NEG_INF).astype(F32)
    bias = jnp.concatenate([bias1] * GROUP, axis=0)
    lane_head = lax.broadcasted_iota(jnp.int32, (nk, KV_WIDTH), 1) // HEAD_DIM
    acc = jnp.zeros((GROUP * BLOCK, KV_WIDTH), F32)
    for h in range(N_KV_HEADS):
        hm = lane_head == h
        km = jnp.where(hm, kw, jnp.zeros_like(kw))
        vm = jnp.where(hm, vw, jnp.zeros_like(vw))
        s = lax.dot_general(qs, km, _NT, preferred_element_type=F32) + bias
        sink = jnp.concatenate(
            [jnp.full((BLOCK, 1), sink_ref[h * GROUP + g], F32) for g in range(GROUP)], axis=0)
        m = jnp.maximum(jnp.max(s, axis=-1, keepdims=True), sink)
        p = jnp.exp(s - m)
        denom = jnp.sum(p, axis=-1, keepdims=True) + jnp.exp(sink - m)
        pn = (p * (1.0 / denom)).astype(BF16)
        acc = acc + jnp.dot(pn, vm, preferred_element_type=F32)
    o_ref[...] = acc.reshape(GROUP, BLOCK, KV_WIDTH).astype(BF16)


def _attention(sink, q, k, v, kx, vx, batch, seq, ctx_len):
    nb = seq // BLOCK
    t = batch * seq

    def cur(b, n):
        return (b * nb + n, 0)

    def prev(b, n):
        return (b * nb + jnp.maximum(n - 1, 0), 0)

    def nxt(b, n):
        return (b * nb + jnp.minimum(n + 1, nb - 1), 0)

    kv = lambda f: pl.BlockSpec((BLOCK, KV_WIDTH), f)
    cx = pl.BlockSpec((ctx_len, KV_WIDTH), lambda b, n: (b, 0))
    qo = pl.BlockSpec((GROUP, BLOCK, KV_WIDTH), lambda b, n: (0, b * nb + n, 0))
    return pl.pallas_call(
        functools.partial(_attn_kernel, nb=nb),
        grid=(batch, nb),
        in_specs=[pl.BlockSpec(memory_space=pltpu.SMEM), qo,
                  kv(prev), kv(cur), kv(nxt), kv(prev), kv(cur), kv(nxt), cx, cx],
        out_specs=qo,
        out_shape=jax.ShapeDtypeStruct((GROUP, t, KV_WIDTH), BF16),
        compiler_params=_params(2),
        name="attn",
    )(sink, q, k, k, k, v, v, v, kx, vx)


def _conv_kernel(up_ref, uc_ref, un_ref, dw_ref, dwb_ref, lng_ref, lnb_ref, o_ref, xp_ref,
                 *, tiles_per_seq, tc, rows):
    i = pl.program_id(0)
    first = (i % tiles_per_seq) == 0
    last = (i % tiles_per_seq) == tiles_per_seq - 1
    xp_ref[0:HALO, :] = jnp.where(first, 0.0, up_ref[...])
    xp_ref[HALO:HALO + tc, :] = uc_ref[...]
    xp_ref[HALO + tc:2 * HALO + tc, :] = jnp.where(last, 0.0, un_ref[...])
    for c in range(tc // rows):
        acc = jnp.zeros((rows, CONV_CH), F32)
        for k in range(CONV_WIDTH):
            r0 = c * rows + k + HALO - CONV_PAD
            acc = acc + xp_ref[r0:r0 + rows, :] * dw_ref[k:k + 1, :]
        acc = acc + dwb_ref[...]
        mu = jnp.mean(acc, axis=-1, keepdims=True)
        xc = acc - mu
        var = jnp.mean(xc * xc, axis=-1, keepdims=True)
        y = xc * lax.rsqrt(var + NORM_EPS) * lng_ref[...] + lnb_ref[...]
        o_ref[c * rows:(c + 1) * rows, :] = (y * _sigmoid(y)).astype(BF16)


def _conv(u, dw, dwb, lng, lnb, seq, tc, rows=64):
    t, ch = u.shape
    per = tc // HALO
    nh = t // HALO
    return pl.pallas_call(
        functools.partial(_conv_kernel, tiles_per_seq=seq // tc, tc=tc, rows=rows),
        grid=(t // tc,),
        in_specs=[pl.BlockSpec((HALO, ch), lambda i: (jnp.maximum(i * per - 1, 0), 0)),
                  pl.BlockSpec((tc, ch), lambda i: (i, 0)),
                  pl.BlockSpec((HALO, ch), lambda i: (jnp.minimum((i + 1) * per, nh - 1), 0)),
                  pl.BlockSpec((CONV_WIDTH, ch), lambda i: (0, 0)),
                  pl.BlockSpec((1, ch), lambda i: (0, 0)),
                  pl.BlockSpec((1, ch), lambda i: (0, 0)),
                  pl.BlockSpec((1, ch), lambda i: (0, 0))],
        out_specs=pl.BlockSpec((tc, ch), lambda i: (i, 0)),
        out_shape=jax.ShapeDtypeStruct((t, ch), BF16),
        scratch_shapes=[pltpu.VMEM((tc + 2 * HALO, ch), F32)],
        compiler_params=_params(1),
        name="conv",
    )(u, u, u, dw, dwb, lng, lnb)


def _merge_kernel(o_ref, hc_ref, gate_ref, x_ref, gt_ref, sh_ref, sc_ref, gpost_ref, gpre_ref,
                  woa_ref, woc_ref, wout_ref, x1_ref, h2_ref):
    d = x_ref.shape[1]
    y_attn = jnp.dot(o_ref[0], woa_ref[0:KV_WIDTH, :], preferred_element_type=F32)
    for g in range(1, GROUP):
        y_attn = y_attn + jnp.dot(o_ref[g], woa_ref[g * KV_WIDTH:(g + 1) * KV_WIDTH, :],
                                  preferred_element_type=F32)
    y_conv = jnp.dot(hc_ref[...], woc_ref[...], preferred_element_type=F32)
    mix = (gate_ref[:, :d].astype(F32) * y_attn + gate_ref[:, d:].astype(F32) * y_conv).astype(BF16)
    mixed = jnp.dot(mix, wout_ref[...], preferred_element_type=F32)
    x1 = x_ref[...] + gt_ref[0] * _rms(mixed, gpost_ref[...])
    x1_ref[...] = x1
    h2_ref[...] = (_rms(x1, gpre_ref[...]) * (1.0 + sc_ref[0]) + sh_ref[0]).astype(BF16)


def _merge(o, hc, gates, x2, gt1, sh2, sc2, gpost, gpre, woa, woc, wout, seq, tm):
    t, d = x2.shape
    mod = pl.BlockSpec((1, 1, d), lambda i: ((i * tm) // seq, 0, 0))
    vec = pl.BlockSpec((1, d), lambda i: (0, 0))
    full = lambda a: pl.BlockSpec(a.shape, lambda i: (0, 0))
    return pl.pallas_call(
        _merge_kernel,
        grid=(t // tm,),
        in_specs=[pl.BlockSpec((GROUP, tm, KV_WIDTH), lambda i: (0, i, 0)),
                  pl.BlockSpec((tm, CONV_CH), lambda i: (i, 0)),
                  pl.BlockSpec((tm, 2 * d), lambda i: (i, 0)),
                  pl.BlockSpec((tm, d), lambda i: (i, 0)),
                  mod, mod, mod, vec, vec, full(woa), full(woc), full(wout)],
        out_specs=[pl.BlockSpec((tm, d), lambda i: (i, 0)),
                   pl.BlockSpec((tm, d), lambda i: (i, 0))],
        out_shape=[jax.ShapeDtypeStruct((t, d), F32),
                   jax.ShapeDtypeStruct((t, d), BF16)],
        compiler_params=_params(1),
        name="merge",
    )(o, hc, gates, x2, gt1, sh2, sc2, gpost, gpre, woa, woc, wout)


def _top16(s):
    rem = s
    rank = jnp.full(s.shape, float(PEER_TOPK), F32)
    vals = []
    for j in range(PEER_TOPK):
        m = jnp.max(rem, axis=0, keepdims=True)
        eq = rem == m
        rank = jnp.where(eq, float(j), rank)
        rem = jnp.where(eq, NEG_BIG, rem)
        vals.append(m)
    return jnp.concatenate(vals, axis=0), rank


def _kth_largest(c, k):
    rem = c
    cnt = jnp.zeros((1, c.shape[1]), F32)
    thr = jnp.full((1, c.shape[1]), NEG_BIG, F32)
    for _ in range(k):
        m = jnp.max(rem, axis=0, keepdims=True)
        eq = rem == m
        thr = jnp.where(cnt < float(k), m, thr)
        cnt = cnt + jnp.sum(jnp.where(eq, 1.0, 0.0), axis=0, keepdims=True)
        rem = jnp.where(eq, NEG_BIG, rem)
    return thr


def _score_kernel(h2_ref, wqt_ref, keys_ref, n_ref, e1_ref, r2_ref, e2_ref):
    qpt = lax.dot_general(wqt_ref[...], h2_ref[...], _NT, preferred_element_type=F32)
    for h in range(PEER_HEADS):
        q1 = qpt[h * PEER_QDIM:h * PEER_QDIM + PEER_HALF].astype(BF16)
        q2 = qpt[h * PEER_QDIM + PEER_HALF:(h + 1) * PEER_QDIM].astype(BF16)
        s1 = jnp.dot(keys_ref[h, 0], q1, preferred_element_type=F32)
        s2 = jnp.dot(keys_ref[h, 1], q2, preferred_element_type=F32)
        v1, rank1 = _top16(s1)
        v2, rank2 = _top16(s2)
        half = PEER_TOPK // 2
        cand = jnp.concatenate([v1[0:1] + v2]
                               + [v1[i:i + 1] + v2[0:half] for i in range(1, half)]
                               + [v1[half:] + v2[0:1]], axis=0)
        thr = _kth_largest(cand, PEER_TOPK)
        in_top1 = rank1 < float(PEER_TOPK)
        n = jnp.zeros(s1.shape, F32)
        for j in range(PEER_TOPK):
            n = n + jnp.where(s1 + v2[j:j + 1] >= thr, 1.0, 0.0)
        ev1 = jnp.exp(v1 - v1[0:1])
        ev2 = jnp.exp(v2 - v2[0:1])
        z = jnp.zeros_like(thr)
        for i in range(PEER_TOPK):
            sel = (v1[i:i + 1] + v2) >= thr
            z = z + ev1[i:i + 1] * jnp.sum(jnp.where(sel, ev2, 0.0), axis=0, keepdims=True)
        n_ref[h] = jnp.where(in_top1, n, 0.0)
        e1_ref[h] = jnp.where(in_top1, jnp.exp(s1 - v1[0:1]) * (1.0 / z), 0.0)
        r2_ref[h] = rank2
        e2_ref[h] = jnp.exp(s2 - v2[0:1])


def _score(h2, wqt, keys, tt):
    t, d = h2.shape
    blk = pl.BlockSpec((PEER_HEADS, N_KEYS, tt), lambda j: (0, 0, j))
    shp = jax.ShapeDtypeStruct((PEER_HEADS, N_KEYS, t), F32)
    return pl.pallas_call(
        _score_kernel,
        grid=(t // tt,),
        in_specs=[pl.BlockSpec((tt, d), lambda j: (j, 0)),
                  pl.BlockSpec(wqt.shape, lambda j: (0, 0)),
                  pl.BlockSpec(keys.shape, lambda j: (0, 0, 0, 0))],
        out_specs=[blk, blk, blk, blk],
        out_shape=[shp, shp, shp, shp],
        compiler_params=_params(1),
        name="peer_score",
    )(h2, wqt, keys)


def _expert_kernel(h2_ref, u_ref, vt_ref, n_ref, e1_ref, r2_ref, e2_ref, x1_ref, gt_ref, gpost_ref,
                   o_ref, acc_ref, ga_ref, *, eb, ec):
    i = pl.program_id(1)

    @pl.when(i == 0)
    def _():
        acc_ref[...] = jnp.zeros_like(acc_ref)

    h2 = h2_ref[...]
    per = ec // N_KEYS
    for c in range(eb // ec):
        at = lax.dot_general(u_ref[c * ec:(c + 1) * ec, :], h2, _NT, preferred_element_type=F32)
        act = 0.5 * at * (1.0 + lax.erf(at * SQRT_HALF))
        for aa in range(per):
            al = c * per + aa
            g = jnp.zeros((N_KEYS, at.shape[1]), F32)
            for h in range(PEER_HEADS):
                nrow = n_ref[h, al:al + 1, :]
                erow = e1_ref[h, al:al + 1, :]
                g = g + jnp.where(r2_ref[h] < nrow, e2_ref[h] * erow, 0.0)
            ga_ref[al * N_KEYS:(al + 1) * N_KEYS, :] = (g * act[aa * N_KEYS:(aa + 1) * N_KEYS]).astype(BF16)
    acc_ref[...] += jnp.dot(vt_ref[...], ga_ref[...], preferred_element_type=F32)

    @pl.when(i == pl.num_programs(1) - 1)
    def _():
        y = acc_ref[...].T
        o_ref[...] = x1_ref[...] + gt_ref[0] * _rms(y, gpost_ref[...])


def _experts(h2, u, vt, n, e1, r2, e2, x1, gt2, gpost, seq, tt, eb, ec=256):
    t, d = h2.shape
    ne = u.shape[0]
    a = eb // N_KEYS
    sub = pl.BlockSpec((PEER_HEADS, a, tt), lambda j, i: (0, i, j))
    allk = pl.BlockSpec((PEER_HEADS, N_KEYS, tt), lambda j, i: (0, 0, j))
    return pl.pallas_call(
        functools.partial(_expert_kernel, eb=eb, ec=ec),
        grid=(t // tt, ne // eb),
        in_specs=[pl.BlockSpec((tt, d), lambda j, i: (j, 0)),
                  pl.BlockSpec((eb, d), lambda j, i: (i, 0)),
                  pl.BlockSpec((d, eb), lambda j, i: (0, i)),
                  sub, sub, allk, allk,
                  pl.BlockSpec((tt, d), lambda j, i: (j, 0)),
                  pl.BlockSpec((1, 1, d), lambda j, i: ((j * tt) // seq, 0, 0)),
                  pl.BlockSpec((1, d), lambda j, i: (0, 0))],
        out_specs=pl.BlockSpec((tt, d), lambda j, i: (j, 0)),
        out_shape=jax.ShapeDtypeStruct((t, d), F32),
        scratch_shapes=[pltpu.VMEM((d, tt), F32), pltpu.VMEM((eb, tt), BF16)],
        compiler_params=_params(2),
        name="peer_experts",
    )(h2, u, vt, n, e1, r2, e2, x1, gt2, gpost)


def _rope_table(seq):
    t = jnp.arange(seq, dtype=jnp.int32)
    row = (t // GRID_W).astype(F32)
    col = (t % GRID_W).astype(F32)
    freqs = ROPE_BASE ** (-jnp.arange(0, ROPE_AXIS_DIM, 2, dtype=F32) / ROPE_AXIS_DIM)
    ang_r = row[:, None] * freqs[None, :]
    ang_c = col[:, None] * freqs[None, :]
    zero = jnp.zeros_like(ang_r)
    cos = jnp.concatenate([jnp.cos(ang_r), jnp.cos(ang_r), jnp.cos(ang_c), jnp.cos(ang_c)], axis=1)
    s_lo = jnp.concatenate([-jnp.sin(ang_r), zero, -jnp.sin(ang_c), zero], axis=1)
    s_hi = jnp.concatenate([zero, jnp.sin(ang_r), zero, jnp.sin(ang_c)], axis=1)
    two = lambda a: jnp.concatenate([a, a], axis=1)
    return jnp.concatenate([two(cos), two(s_lo), two(s_hi)], axis=1)


def _layer(x2, ctx2, cond, rope_tab, batch, seq, ctx_len,
           w_ada, b_ada, g_pre_mix, g_post_mix, g_pre_ffn, g_post_ffn, w_in, attn_sink, w_o_attn,
           conv_dw, conv_dw_b, conv_ln_g, conv_ln_b, w_o_conv, w_out, peer_wq, peer_keys, peer_u, peer_v):
    d = D_MODEL
    t = batch * seq
    tm = min(512, seq)
    row = lambda v: v.reshape(1, -1)

    mod = _ada(cond, w_ada, b_ada)
    per_batch = [mod[:batch, j * d:(j + 1) * d].reshape(batch, 1, d) for j in range(6)]
    sh1, sc1, gt1, sh2, sc2, gt2 = per_batch
    csh1 = mod[batch:batch + 1, 0:d]
    csc1 = mod[batch:batch + 1, d:2 * d]

    wq = w_in[:, :K_OFF].reshape(d, N_KV_HEADS, GROUP, HEAD_DIM).transpose(0, 2, 1, 3).reshape(d, ATTN_WIDTH)
    w_in_b = jnp.concatenate([wq, w_in[:, K_OFF:]], axis=1).astype(BF16)
    woa = w_o_attn.reshape(N_KV_HEADS, GROUP, HEAD_DIM, d).transpose(1, 0, 2, 3).reshape(ATTN_WIDTH, d).astype(BF16)

    q, k, v, u, gates = _in_proj(x2, sh1, sc1, row(g_pre_mix), rope_tab, w_in_b, seq, tm)
    kx, vx = _ctx_proj(ctx2, csh1, csc1, row(g_pre_mix), w_in_b[:, K_OFF:CONV_OFF], min(512, ctx2.shape[0]))
    o = _attention(attn_sink, q, k, v, kx, vx, batch, seq, ctx_len)
    hc = _conv(u, conv_dw, row(conv_dw_b), row(conv_ln_g), row(conv_ln_b), seq, min(256, seq))
    x1, h2 = _merge(o, hc, gates, x2, gt1, sh2, sc2, row(g_post_mix), row(g_pre_ffn),
                    woa, w_o_conv.astype(BF16), w_out.astype(BF16), seq, tm)
    n, e1, r2, e2 = _score(h2, peer_wq.T.astype(BF16), peer_keys.astype(BF16), min(256, seq))
    return _experts(h2, peer_u.astype(BF16), peer_v.T.astype(BF16), n, e1, r2, e2, x1, gt2,
                    row(g_post_ffn), seq, min(512, seq), 1024)


def kernel(x, c, ctx, c_ctx, w_ada, b_ada, g_pre_mix, g_post_mix, g_pre_ffn, g_post_ffn, w_in, attn_sink,
           w_o_attn, conv_dw, conv_dw_b, conv_ln_g, conv_ln_b, w_o_conv, w_out, peer_wq, peer_keys, peer_u,
           peer_v):
    batch, seq, d = x.shape
    ctx_len = ctx.shape[1]
    depth = w_ada.shape[0]
    assert depth == 1, "context-stream update between layers is not implemented"
    assert d == D_MODEL and seq % BLOCK == 0 and seq % GRID_W == 0 and batch < 16
    cond = jnp.zeros((16, d), F32).at[:batch].set(c).at[batch].set(c_ctx)
    rope_tab = _rope_table(seq)
    out = _layer(x.reshape(batch * seq, d), ctx.reshape(batch * ctx_len, d), cond, rope_tab, batch, seq, ctx_len,
                 w_ada[0], b_ada[0], g_pre_mix[0], g_post_mix[0], g_pre_ffn[0], g_post_ffn[0], w_in[0],
                 attn_sink[0], w_o_attn[0], conv_dw[0], conv_dw_b[0], conv_ln_g[0], conv_ln_b[0], w_o_conv[0],
                 w_out[0], peer_wq[0], peer_keys[0], peer_u[0], peer_v[0])
    return out.reshape(batch, seq, d)
```

```python
import functools

import jax
import jax.numpy as jnp
from jax import lax
from jax.experimental import pallas as pl
from jax.experimental.pallas import tpu as pltpu

F32 = jnp.float32
BF16 = jnp.bfloat16

D_MODEL = 1024
GRID_W = 64
N_HEADS = 16
N_KV_HEADS = 4
GROUP = N_HEADS // N_KV_HEADS
HEAD_DIM = 64
ATTN_WIDTH = N_HEADS * HEAD_DIM
KV_WIDTH = N_KV_HEADS * HEAD_DIM
WINDOW = 128
BLOCK = 128
ROPE_AXIS_DIM = HEAD_DIM // 2
ROPE_BASE = 10000.0
CONV_CH = D_MODEL // 2
CONV_WIDTH = 31
CONV_PAD = CONV_WIDTH // 2
K_OFF = ATTN_WIDTH
V_OFF = K_OFF + KV_WIDTH
CONV_OFF = V_OFF + KV_WIDTH
GATE_OFF = CONV_OFF + 2 * CONV_CH
IN_WIDTH = GATE_OFF + 2 * D_MODEL
PEER_HEADS = 8
N_KEYS = 128
N_EXPERTS = N_KEYS * N_KEYS
PEER_QDIM = 256
PEER_HALF = PEER_QDIM // 2
PEER_TOPK = 16
NORM_EPS = 1e-6
NEG_INF = -1e30
NEG_BIG = -3.0e38
SQRT_HALF = 0.7071067811865476
Q_SCALE = HEAD_DIM ** -0.5

LANES = 128
BF16_ROWS = 16
HALO = 16
GATE_LANES = 256
VMEM_LIMIT = 56 * 1024 * 1024

_NT = (((1,), (1,)), ((), ()))


def _params(n_axes, vmem=VMEM_LIMIT, flags=None):
    return pltpu.CompilerParams(dimension_semantics=("arbitrary",) * n_axes, vmem_limit_bytes=vmem, flags=flags)


def _rms(x, g):
    return x * lax.rsqrt(jnp.mean(x * x, axis=-1, keepdims=True) + NORM_EPS) * g


def _sigmoid(x):
    return jax.nn.sigmoid(x)


def _ada_kernel(c_ref, w_ref, b_ref, o_ref):
    c = c_ref[...]
    s = (c * _sigmoid(c)).astype(BF16)
    o_ref[...] = jnp.dot(s, w_ref[...].astype(BF16), preferred_element_type=F32) + b_ref[...]


def _ada(cond, w_ada, b_ada):
    rows, d = cond.shape
    n = w_ada.shape[1]
    bn = 1536
    return pl.pallas_call(
        _ada_kernel,
        grid=(n // bn,),
        in_specs=[pl.BlockSpec((rows, d), lambda j: (0, 0)),
                  pl.BlockSpec((d, bn), lambda j: (0, j)),
                  pl.BlockSpec((1, bn), lambda j: (0, j))],
        out_specs=pl.BlockSpec((rows, bn), lambda j: (0, j)),
        out_shape=jax.ShapeDtypeStruct((rows, n), F32),
        compiler_params=_params(1),
        name="ada",
    )(cond, w_ada, b_ada.reshape(1, n))


def _in_kernel(x_ref, sh_ref, sc_ref, g_ref, tab_ref, w_ref, q_ref, k_ref, v_ref, u_ref, gate_ref):
    h = (_rms(x_ref[...], g_ref[...]) * (1.0 + sc_ref[0]) + sh_ref[0]).astype(BF16)
    cos = tab_ref[:, 0:LANES]
    sin_lo = tab_ref[:, LANES:2 * LANES]
    sin_hi = tab_ref[:, 2 * LANES:3 * LANES]

    def proj(lo, n):
        return jnp.dot(h, w_ref[:, lo:lo + n], preferred_element_type=F32)

    def rope(p):
        return (p * cos + pltpu.roll(p, LANES - ROPE_AXIS_DIM // 2, 1) * sin_lo
                + pltpu.roll(p, ROPE_AXIS_DIM // 2, 1) * sin_hi)

    def rope2(p):
        return jnp.concatenate([rope(p[:, :LANES]), rope(p[:, LANES:])], axis=1)

    for g in range(GROUP):
        q_ref[g] = (rope2(proj(g * KV_WIDTH, KV_WIDTH)) * Q_SCALE).astype(BF16)
    k_ref[...] = rope2(proj(K_OFF, KV_WIDTH)).astype(BF16)
    v_ref[...] = proj(V_OFF, KV_WIDTH).astype(BF16)
    u_ref[...] = proj(CONV_OFF, CONV_CH) * _sigmoid(proj(CONV_OFF + CONV_CH, CONV_CH))
    for j in range(4):
        gate_ref[:, j * 512:(j + 1) * 512] = _sigmoid(proj(GATE_OFF + j * 512, 512)).astype(BF16)


def _in_proj(x2, sh, sc, g, tab, w, seq, tm):
    t, d = x2.shape
    return pl.pallas_call(
        _in_kernel,
        grid=(t // tm,),
        in_specs=[pl.BlockSpec((tm, d), lambda i: (i, 0)),
                  pl.BlockSpec((1, 1, d), lambda i: ((i * tm) // seq, 0, 0)),
                  pl.BlockSpec((1, 1, d), lambda i: ((i * tm) // seq, 0, 0)),
                  pl.BlockSpec((1, d), lambda i: (0, 0)),
                  pl.BlockSpec((tm, 3 * LANES), lambda i: (i % (seq // tm), 0)),
                  pl.BlockSpec((d, IN_WIDTH), lambda i: (0, 0))],
        out_specs=[pl.BlockSpec((GROUP, tm, KV_WIDTH), lambda i: (0, i, 0)),
                   pl.BlockSpec((tm, KV_WIDTH), lambda i: (i, 0)),
                   pl.BlockSpec((tm, KV_WIDTH), lambda i: (i, 0)),
                   pl.BlockSpec((tm, CONV_CH), lambda i: (i, 0)),
                   pl.BlockSpec((tm, 2 * d), lambda i: (i, 0))],
        out_shape=[jax.ShapeDtypeStruct((GROUP, t, KV_WIDTH), BF16),
                   jax.ShapeDtypeStruct((t, KV_WIDTH), BF16),
                   jax.ShapeDtypeStruct((t, KV_WIDTH), BF16),
                   jax.ShapeDtypeStruct((t, CONV_CH), F32),
                   jax.ShapeDtypeStruct((t, 2 * d), BF16)],
        compiler_params=_params(1),
        name="in_proj",
    )(x2, sh, sc, g, tab, w)


def _ctx_kernel(x_ref, sh_ref, sc_ref, g_ref, w_ref, k_ref, v_ref):
    h = (_rms(x_ref[...], g_ref[...]) * (1.0 + sc_ref[...]) + sh_ref[...]).astype(BF16)
    k_ref[...] = jnp.dot(h, w_ref[:, :KV_WIDTH], preferred_element_type=F32).astype(BF16)
    v_ref[...] = jnp.dot(h, w_ref[:, KV_WIDTH:], preferred_element_type=F32).astype(BF16)


def _ctx_proj(c2, sh, sc, g, w, tm):
    t, d = c2.shape
    return pl.pallas_call(
        _ctx_kernel,
        grid=(t // tm,),
        in_specs=[pl.BlockSpec((tm, d), lambda i: (i, 0)),
                  pl.BlockSpec((1, d), lambda i: (0, 0)),
                  pl.BlockSpec((1, d), lambda i: (0, 0)),
                  pl.BlockSpec((1, d), lambda i: (0, 0)),
                  pl.BlockSpec((d, 2 * KV_WIDTH), lambda i: (0, 0))],
        out_specs=[pl.BlockSpec((tm, KV_WIDTH), lambda i: (i, 0)),
                   pl.BlockSpec((tm, KV_WIDTH), lambda i: (i, 0))],
        out_shape=[jax.ShapeDtypeStruct((t, KV_WIDTH), BF16),
                   jax.ShapeDtypeStruct((t, KV_WIDTH), BF16)],
        compiler_params=_params(1),
        name="ctx_proj",
    )(c2, sh, sc, g, w)


def _attn_kernel(sink_ref, q_ref, kp_ref, kc_ref, kn_ref, vp_ref, vc_ref, vn_ref, kx_ref, vx_ref,
                 o_ref, *, nb):
    n = pl.program_id(1)
    qs = q_ref[...].reshape(GROUP * BLOCK, KV_WIDTH)
    kw = jnp.concatenate([kp_ref[...], kc_ref[...], kn_ref[...], kx_ref[...]], axis=0)
    vw = jnp.concatenate([vp_ref[...], vc_ref[...], vn_ref[...], vx_ref[...]], axis=0)
    nk = kw.shape[0]
    qi = lax.broadcasted_iota(jnp.int32, (BLOCK, nk), 0)
    kk = lax.broadcasted_iota(jnp.int32, (BLOCK, nk), 1)
    lo = jnp.where(n == 0, BLOCK, 0)
    hi = jnp.where(n == nb - 1, 2 * BLOCK, 3 * BLOCK)
    in_win = (kk >= qi) & (kk <= qi + 2 * WINDOW) & (kk >= lo) & (kk < hi)
    bias1 = jnp.where(in_win | (kk >= 3 * BLOCK), 0.0, NEG_INF).astype(F32)
    bias = jnp.concatenate([bias1] * GROUP, axis=0)
    lane_head = lax.broadcasted_iota(jnp.int32, (nk, KV_WIDTH), 1) // HEAD_DIM
    acc = jnp.zeros((GROUP * BLOCK, KV_WIDTH), F32)
    for h in range(N_KV_HEADS):
        hm = lane_head == h
        km = jnp.where(hm, kw, jnp.zeros_like(kw))
        vm = jnp.where(hm, vw, jnp.zeros_like(vw))
        s = lax.dot_general(qs, km, _NT, preferred_element_type=F32) + bias
        sink = jnp.concatenate(
            [jnp.full((BLOCK, 1), sink_ref[h * GROUP + g], F32) for g in range(GROUP)], axis=0)
        m = jnp.maximum(jnp.max(s, axis=-1, keepdims=True), sink)
        p = jnp.exp(s - m)
        denom = jnp.sum(p, axis=-1, keepdims=True) + jnp.exp(sink - m)
        pn = (p * (1.0 / denom)).astype(BF16)
        acc = acc + jnp.dot(pn, vm, preferred_element_type=F32)
    o_ref[...] = acc.reshape(GROUP, BLOCK, KV_WIDTH).astype(BF16)


def _attention(sink, q, k, v, kx, vx, batch, seq, ctx_len):
    nb = seq // BLOCK
    t = batch * seq

    def cur(b, n):
        return (b * nb + n, 0)

    def prev(b, n):
        return (b * nb + jnp.maximum(n - 1, 0), 0)

    def nxt(b, n):
        return (b * nb + jnp.minimum(n + 1, nb - 1), 0)

    kv = lambda f: pl.BlockSpec((BLOCK, KV_WIDTH), f)
    cx = pl.BlockSpec((ctx_len, KV_WIDTH), lambda b, n: (b, 0))
    qo = pl.BlockSpec((GROUP, BLOCK, KV_WIDTH), lambda b, n: (0, b * nb + n, 0))
    return pl.pallas_call(
        functools.partial(_attn_kernel, nb=nb),
        grid=(batch, nb),
        in_specs=[pl.BlockSpec(memory_space=pltpu.SMEM), qo,
                  kv(prev), kv(cur), kv(nxt), kv(prev), kv(cur), kv(nxt), cx, cx],
        out_specs=qo,
        out_shape=jax.ShapeDtypeStruct((GROUP, t, KV_WIDTH), BF16),
        compiler_params=_params(2),
        name="attn",
    )(sink, q, k, k, k, v, v, v, kx, vx)


def _conv_kernel(up_ref, uc_ref, un_ref, dw_ref, dwb_ref, lng_ref, lnb_ref, o_ref, xp_ref,
                 *, tiles_per_seq, tc, rows):
    i = pl.program_id(0)
    first = (i % tiles_per_seq) == 0
    last = (i % tiles_per_seq) == tiles_per_seq - 1
    xp_ref[0:HALO, :] = jnp.where(first, 0.0, up_ref[...])
    xp_ref[HALO:HALO + tc, :] = uc_ref[...]
    xp_ref[HALO + tc:2 * HALO + tc, :] = jnp.where(last, 0.0, un_ref[...])
    for c in range(tc // rows):
        acc = jnp.zeros((rows, CONV_CH), F32)
        for k in range(CONV_WIDTH):
            r0 = c * rows + k + HALO - CONV_PAD
            acc = acc + xp_ref[r0:r0 + rows, :] * dw_ref[k:k + 1, :]
        acc = acc + dwb_ref[...]
        mu = jnp.mean(acc, axis=-1, keepdims=True)
        xc = acc - mu
        var = jnp.mean(xc * xc, axis=-1, keepdims=True)
        y = xc * lax.rsqrt(var + NORM_EPS) * lng_ref[...] + lnb_ref[...]
        o_ref[c * rows:(c + 1) * rows, :] = (y * _sigmoid(y)).astype(BF16)


def _conv(u, dw, dwb, lng, lnb, seq, tc, rows=64):
    t, ch = u.shape
    per = tc // HALO
    nh = t // HALO
    return pl.pallas_call(
        functools.partial(_conv_kernel, tiles_per_seq=seq // tc, tc=tc, rows=rows),
        grid=(t // tc,),
        in_specs=[pl.BlockSpec((HALO, ch), lambda i: (jnp.maximum(i * per - 1, 0), 0)),
                  pl.BlockSpec((tc, ch), lambda i: (i, 0)),
                  pl.BlockSpec((HALO, ch), lambda i: (jnp.minimum((i + 1) * per, nh - 1), 0)),
                  pl.BlockSpec((CONV_WIDTH, ch), lambda i: (0, 0)),
                  pl.BlockSpec((1, ch), lambda i: (0, 0)),
                  pl.BlockSpec((1, ch), lambda i: (0, 0)),
                  pl.BlockSpec((1, ch), lambda i: (0, 0))],
        out_specs=pl.BlockSpec((tc, ch), lambda i: (i, 0)),
        out_shape=jax.ShapeDtypeStruct((t, ch), BF16),
        scratch_shapes=[pltpu.VMEM((tc + 2 * HALO, ch), F32)],
        compiler_params=_params(1),
        name="conv",
    )(u, u, u, dw, dwb, lng, lnb)


def _merge_kernel(o_ref, hc_ref, gate_ref, x_ref, gt_ref, sh_ref, sc_ref, gpost_ref, gpre_ref,
                  woa_ref, woc_ref, wout_ref, x1_ref, h2t_ref):
    d = x_ref.shape[1]
    y_attn = jnp.dot(o_ref[0], woa_ref[0:KV_WIDTH, :], preferred_element_type=F32)
    for g in range(1, GROUP):
        y_attn = y_attn + jnp.dot(o_ref[g], woa_ref[g * KV_WIDTH:(g + 1) * KV_WIDTH, :],
                                  preferred_element_type=F32)
    y_conv = jnp.dot(hc_ref[...], woc_ref[...], preferred_element_type=F32)
    mix = (gate_ref[:, :d].astype(F32) * y_attn + gate_ref[:, d:].astype(F32) * y_conv).astype(BF16)
    mixed = jnp.dot(mix, wout_ref[...], preferred_element_type=F32)
    x1 = x_ref[...] + gt_ref[0] * _rms(mixed, gpost_ref[...])
    x1_ref[...] = x1
    h2 = _rms(x1, gpre_ref[...]) * (1.0 + sc_ref[0]) + sh_ref[0]
    h2t_ref[...] = h2.T.astype(BF16)


def _merge(o, hc, gates, x2, gt1, sh2, sc2, gpost, gpre, woa, woc, wout, seq, tm):
    t, d = x2.shape
    mod = pl.BlockSpec((1, 1, d), lambda i: ((i * tm) // seq, 0, 0))
    vec = pl.BlockSpec((1, d), lambda i: (0, 0))
    full = lambda a: pl.BlockSpec(a.shape, lambda i: (0, 0))
    return pl.pallas_call(
        _merge_kernel,
        grid=(t // tm,),
        in_specs=[pl.BlockSpec((GROUP, tm, KV_WIDTH), lambda i: (0, i, 0)),
                  pl.BlockSpec((tm, CONV_CH), lambda i: (i, 0)),
                  pl.BlockSpec((tm, 2 * d), lambda i: (i, 0)),
                  pl.BlockSpec((tm, d), lambda i: (i, 0)),
                  mod, mod, mod, vec, vec, full(woa), full(woc), full(wout)],
        out_specs=[pl.BlockSpec((tm, d), lambda i: (i, 0)),
                   pl.BlockSpec((d, tm), lambda i: (0, i))],
        out_shape=[jax.ShapeDtypeStruct((t, d), F32),
                   jax.ShapeDtypeStruct((d, t), BF16)],
        compiler_params=_params(1),
        name="merge",
    )(o, hc, gates, x2, gt1, sh2, sc2, gpost, gpre, woa, woc, wout)


def _top16(s):
    rem = s
    rank = jnp.full(s.shape, float(PEER_TOPK), F32)
    vals = []
    for j in range(PEER_TOPK):
        m = jnp.max(rem, axis=0, keepdims=True)
        eq = rem == m
        rank = jnp.where(eq, float(j), rank)
        rem = jnp.where(eq, NEG_BIG, rem)
        vals.append(m)
    return jnp.concatenate(vals, axis=0), rank


def _kth_largest(c, k):
    rem = c
    cnt = jnp.zeros((1, c.shape[1]), F32)
    thr = jnp.full((1, c.shape[1]), NEG_BIG, F32)
    for _ in range(k):
        m = jnp.max(rem, axis=0, keepdims=True)
        eq = rem == m
        thr = jnp.where(cnt < float(k), m, thr)
        cnt = cnt + jnp.sum(jnp.where(eq, 1.0, 0.0), axis=0, keepdims=True)
        rem = jnp.where(eq, NEG_BIG, rem)
    return thr


def _score_kernel(h2t_ref, wqt_ref, keys_ref, n_ref, e1_ref, r2_ref, e2_ref):
    qpt = jnp.dot(wqt_ref[...], h2t_ref[...], preferred_element_type=F32)
    for h in range(PEER_HEADS):
        q1 = qpt[h * PEER_QDIM:h * PEER_QDIM + PEER_HALF].astype(BF16)
        q2 = qpt[h * PEER_QDIM + PEER_HALF:(h + 1) * PEER_QDIM].astype(BF16)
        s1 = jnp.dot(keys_ref[h, 0], q1, preferred_element_type=F32)
        s2 = jnp.dot(keys_ref[h, 1], q2, preferred_element_type=F32)
        v1, rank1 = _top16(s1)
        v2, rank2 = _top16(s2)
        half = PEER_TOPK // 2
        cand = jnp.concatenate([v1[0:1] + v2]
                               + [v1[i:i + 1] + v2[0:half] for i in range(1, half)]
                               + [v1[half:] + v2[0:1]], axis=0)
        thr = _kth_largest(cand, PEER_TOPK)
        ev1 = jnp.exp(v1 - v1[0:1])
        ev2 = jnp.exp(v2 - v2[0:1])
        z = jnp.zeros_like(thr)
        n = jnp.zeros(s1.shape, F32)
        for i in range(PEER_TOPK):
            sel = (v1[i:i + 1] + v2) >= thr
            cnt = jnp.sum(jnp.where(sel, 1.0, 0.0), axis=0, keepdims=True)
            z = z + ev1[i:i + 1] * jnp.sum(jnp.where(sel, ev2, 0.0), axis=0, keepdims=True)
            n = jnp.where(rank1 == float(i), cnt, n)
        e1 = jnp.where(rank1 < float(PEER_TOPK), jnp.exp(s1 - v1[0:1]) * (0.5 / z), 0.0)
        n_ref[h] = n
        e1_ref[h] = e1
        r2_ref[h] = rank2.astype(BF16)
        e2_ref[h] = jnp.exp(s2 - v2[0:1]).astype(BF16)


def _score(h2t, wqt, keys, tt):
    d, t = h2t.shape
    blk = pl.BlockSpec((PEER_HEADS, N_KEYS, tt), lambda j: (0, 0, j))
    shp = lambda dt: jax.ShapeDtypeStruct((PEER_HEADS, N_KEYS, t), dt)
    return pl.pallas_call(
        _score_kernel,
        grid=(t // tt,),
        in_specs=[pl.BlockSpec((d, tt), lambda j: (0, j)),
                  pl.BlockSpec(wqt.shape, lambda j: (0, 0)),
                  pl.BlockSpec(keys.shape, lambda j: (0, 0, 0, 0))],
        out_specs=[blk, blk, blk, blk],
        out_shape=[shp(F32), shp(F32), shp(BF16), shp(BF16)],
        compiler_params=_params(1),
        name="peer_score",
    )(h2t, wqt, keys)


def _expert_kernel(h2t_ref, u_ref, vt_ref, n_ref, e1_ref, r2_ref, e2_ref, x1_ref, gt_ref, gpost_ref,
                   o_ref, acc_ref, at_ref, ga_ref, *, eb, ec):
    i = pl.program_id(1)

    @pl.when(i == 0)
    def _():
        acc_ref[...] = jnp.zeros_like(acc_ref)

    tt = h2t_ref.shape[1]
    per = ec // N_KEYS
    n_chunks = eb // ec
    pk = N_KEYS // BF16_ROWS

    def up(c, ls):
        at = jnp.dot(u_ref[c * ec:(c + 1) * ec, :], h2t_ref[:, ls], preferred_element_type=F32)
        at_ref[c % 2, :, :, ls] = at.reshape(ec // BF16_ROWS, BF16_ROWS, GATE_LANES)

    def down(c, ls):
        acc_ref[:, ls] += jnp.dot(vt_ref[:, c * ec:(c + 1) * ec],
                                  ga_ref[c % 2, :, :, ls].reshape(ec, GATE_LANES),
                                  preferred_element_type=F32)

    def gate(c, aa, ls):
        al = c * per + aa
        rows = slice(aa * pk, (aa + 1) * pk)
        g = None
        for h in range(PEER_HEADS):
            nrow = jnp.broadcast_to(n_ref[h, al:al + 1, ls], (BF16_ROWS, GATE_LANES)).astype(BF16)
            erow = jnp.broadcast_to(e1_ref[h, al:al + 1, ls], (BF16_ROWS, GATE_LANES)).astype(BF16)
            term = jnp.where(r2_ref[h, :, :, ls] < nrow[None], e2_ref[h, :, :, ls] * erow[None], 0.0)
            g = term if g is None else g + term
        at = at_ref[c % 2, rows, :, ls]
        act = (at * (1.0 + lax.erf(at * SQRT_HALF))).astype(BF16)
        ga_ref[c % 2, rows, :, ls] = g * act

    slabs = [slice(l0, l0 + GATE_LANES) for l0 in range(0, tt, GATE_LANES)]
    for ls in slabs:
        up(0, ls)
    for c in range(n_chunks):
        for ls in slabs:
            gate(c, 0, ls)
            if c + 1 < n_chunks:
                up(c + 1, ls)
            for aa in range(1, per):
                gate(c, aa, ls)
            down(c, ls)

    @pl.when(i == pl.num_programs(1) - 1)
    def _():
        y = acc_ref[...].T
        o_ref[...] = x1_ref[...] + gt_ref[0] * _rms(y, gpost_ref[...])


def _experts(h2t, u, vt, n, e1, r2, e2, x1, gt2, gpost, seq, tt, eb, ec=256):
    d, t = h2t.shape
    ne = u.shape[0]
    a = eb // N_KEYS
    pk = N_KEYS // BF16_ROWS
    sub = pl.BlockSpec((PEER_HEADS, a, tt), lambda j, i: (0, i, j))
    allk = pl.BlockSpec((PEER_HEADS, pk, BF16_ROWS, tt), lambda j, i: (0, 0, 0, j))
    tiled = lambda v: v.reshape(PEER_HEADS, pk, BF16_ROWS, t)
    return pl.pallas_call(
        functools.partial(_expert_kernel, eb=eb, ec=ec),
        grid=(t // tt, ne // eb),
        in_specs=[pl.BlockSpec((d, tt), lambda j, i: (0, j)),
                  pl.BlockSpec((eb, d), lambda j, i: (i, 0)),
                  pl.BlockSpec((d, eb), lambda j, i: (0, i)),
                  sub, sub, allk, allk,
                  pl.BlockSpec((tt, d), lambda j, i: (j, 0)),
                  pl.BlockSpec((1, 1, d), lambda j, i: ((j * tt) // seq, 0, 0)),
                  pl.BlockSpec((1, d), lambda j, i: (0, 0))],
        out_specs=pl.BlockSpec((tt, d), lambda j, i: (j, 0)),
        out_shape=jax.ShapeDtypeStruct((t, d), F32),
        scratch_shapes=[pltpu.VMEM((d, tt), F32),
                        pltpu.VMEM((2, ec // BF16_ROWS, BF16_ROWS, tt), F32),
                        pltpu.VMEM((2, ec // BF16_ROWS, BF16_ROWS, tt), BF16)],
        compiler_params=_params(2),
        name="peer_experts",
    )(h2t, u, vt, n, e1, tiled(r2), tiled(e2), x1, gt2, gpost)


def _rope_table(seq):
    t = jnp.arange(seq, dtype=jnp.int32)
    row = (t // GRID_W).astype(F32)
    col = (t % GRID_W).astype(F32)
    freqs = ROPE_BASE ** (-jnp.arange(0, ROPE_AXIS_DIM, 2, dtype=F32) / ROPE_AXIS_DIM)
    ang_r = row[:, None] * freqs[None, :]
    ang_c = col[:, None] * freqs[None, :]
    zero = jnp.zeros_like(ang_r)
    cos = jnp.concatenate([jnp.cos(ang_r), jnp.cos(ang_r), jnp.cos(ang_c), jnp.cos(ang_c)], axis=1)
    s_lo = jnp.concatenate([-jnp.sin(ang_r), zero, -jnp.sin(ang_c), zero], axis=1)
    s_hi = jnp.concatenate([zero, jnp.sin(ang_r), zero, jnp.sin(ang_c)], axis=1)
    two = lambda a: jnp.concatenate([a, a], axis=1)
    return jnp.concatenate([two(cos), two(s_lo), two(s_hi)], axis=1)


def _layer(x2, ctx2, cond, rope_tab, batch, seq, ctx_len,
           w_ada, b_ada, g_pre_mix, g_post_mix, g_pre_ffn, g_post_ffn, w_in, attn_sink, w_o_attn,
           conv_dw, conv_dw_b, conv_ln_g, conv_ln_b, w_o_conv, w_out, peer_wq, peer_keys, peer_u, peer_v):
    d = D_MODEL
    t = batch * seq
    tm = min(512, seq)
    row = lambda v: v.reshape(1, -1)

    mod = _ada(cond, w_ada, b_ada)
    per_batch = [mod[:batch, j * d:(j + 1) * d].reshape(batch, 1, d) for j in range(6)]
    sh1, sc1, gt1, sh2, sc2, gt2 = per_batch
    csh1 = mod[batch:batch + 1, 0:d]
    csc1 = mod[batch:batch + 1, d:2 * d]

    wq = w_in[:, :K_OFF].reshape(d, N_KV_HEADS, GROUP, HEAD_DIM).transpose(0, 2, 1, 3).reshape(d, ATTN_WIDTH)
    w_in_b = jnp.concatenate([wq, w_in[:, K_OFF:]], axis=1).astype(BF16)
    woa = w_o_attn.reshape(N_KV_HEADS, GROUP, HEAD_DIM, d).transpose(1, 0, 2, 3).reshape(ATTN_WIDTH, d).astype(BF16)

    q, k, v, u, gates = _in_proj(x2, sh1, sc1, row(g_pre_mix), rope_tab, w_in_b, seq, tm)
    kx, vx = _ctx_proj(ctx2, csh1, csc1, row(g_pre_mix), w_in_b[:, K_OFF:CONV_OFF], min(512, ctx2.shape[0]))
    o = _attention(attn_sink, q, k, v, kx, vx, batch, seq, ctx_len)
    hc = _conv(u, conv_dw, row(conv_dw_b), row(conv_ln_g), row(conv_ln_b), seq, min(256, seq))
    x1, h2t = _merge(o, hc, gates, x2, gt1, sh2, sc2, row(g_post_mix), row(g_pre_ffn),
                     woa, w_o_conv.astype(BF16), w_out.astype(BF16), seq, tm)
    n, e1, r2, e2 = _score(h2t, peer_wq.T.astype(BF16), peer_keys.astype(BF16), min(256, seq))
    return _experts(h2t, peer_u.astype(BF16), peer_v.T.astype(BF16), n, e1, r2, e2, x1, gt2,
                    row(g_post_ffn), seq, min(512, seq), 2048)


def kernel(x, c, ctx, c_ctx, w_ada, b_ada, g_pre_mix, g_post_mix, g_pre_ffn, g_post_ffn, w_in, attn_sink,
           w_o_attn, conv_dw, conv_dw_b, conv_ln_g, conv_ln_b, w_o_conv, w_out, peer_wq, peer_keys, peer_u,
           peer_v):
    batch, seq, d = x.shape
    ctx_len = ctx.shape[1]
    depth = w_ada.shape[0]
    assert depth == 1, "context-stream update between layers is not implemented"
    assert d == D_MODEL and seq % BLOCK == 0 and seq % GRID_W == 0 and batch < 16
    cond = jnp.zeros((16, d), F32).at[:batch].set(c).at[batch].set(c_ctx)
    rope_tab = _rope_table(seq)
    out = _layer(x.reshape(batch * seq, d), ctx.reshape(batch * ctx_len, d), cond, rope_tab, batch, seq, ctx_len,
                 w_ada[0], b_ada[0], g_pre_mix[0], g_post_mix[0], g_pre_ffn[0], g_post_ffn[0], w_in[0],
                 attn_sink[0], w_o_attn[0], conv_dw[0], conv_dw_b[0], conv_ln_g[0], conv_ln_b[0], w_o_conv[0],
                 w_out[0], peer_wq[0], peer_keys[0], peer_u[0], peer_v[0])
    return out.reshape(batch, seq, d)
```

```python
import functools

import jax
import jax.numpy as jnp
from jax import lax
from jax.experimental import pallas as pl
from jax.experimental.pallas import tpu as pltpu

F32 = jnp.float32
BF16 = jnp.bfloat16

D_MODEL = 1024
GRID_W = 64
N_HEADS = 16
N_KV_HEADS = 4
GROUP = N_HEADS // N_KV_HEADS
HEAD_DIM = 64
ATTN_WIDTH = N_HEADS * HEAD_DIM
KV_WIDTH = N_KV_HEADS * HEAD_DIM
WINDOW = 128
BLOCK = 128
ROPE_AXIS_DIM = HEAD_DIM // 2
ROPE_BASE = 10000.0
CONV_CH = D_MODEL // 2
CONV_WIDTH = 31
CONV_PAD = CONV_WIDTH // 2
K_OFF = ATTN_WIDTH
V_OFF = K_OFF + KV_WIDTH
CONV_OFF = V_OFF + KV_WIDTH
GATE_OFF = CONV_OFF + 2 * CONV_CH
IN_WIDTH = GATE_OFF + 2 * D_MODEL
PEER_HEADS = 8
N_KEYS = 128
N_EXPERTS = N_KEYS * N_KEYS
PEER_QDIM = 256
PEER_HALF = PEER_QDIM // 2
PEER_TOPK = 16
NORM_EPS = 1e-6
NEG_INF = -1e30
NEG_BIG = -3.0e38
SQRT_HALF = 0.7071067811865476
Q_SCALE = HEAD_DIM ** -0.5
LOG2E = 1.4426950408889634

LANES = 128
SUBLANES = 8
BF16_ROWS = 16
HALO = 16
GATE_LANES = 256
DOWN_CHUNKS = 2
VMEM_LIMIT = 56 * 1024 * 1024

_NT = (((1,), (1,)), ((), ()))


def _params(n_axes, vmem=VMEM_LIMIT, flags=None):
    return pltpu.CompilerParams(dimension_semantics=("arbitrary",) * n_axes, vmem_limit_bytes=vmem, flags=flags)


def _rms(x, g):
    return x * lax.rsqrt(jnp.mean(x * x, axis=-1, keepdims=True) + NORM_EPS) * g


def _sigmoid(x):
    return jax.nn.sigmoid(x)


def _ada_kernel(c_ref, w_ref, b_ref, o_ref):
    c = c_ref[...]
    s = (c * _sigmoid(c)).astype(BF16)
    o_ref[...] = jnp.dot(s, w_ref[...].astype(BF16), preferred_element_type=F32) + b_ref[...]


def _ada(cond, w_ada, b_ada):
    rows, d = cond.shape
    n = w_ada.shape[1]
    bn = 1536
    return pl.pallas_call(
        _ada_kernel,
        grid=(n // bn,),
        in_specs=[pl.BlockSpec((rows, d), lambda j: (0, 0)),
                  pl.BlockSpec((d, bn), lambda j: (0, j)),
                  pl.BlockSpec((1, bn), lambda j: (0, j))],
        out_specs=pl.BlockSpec((rows, bn), lambda j: (0, j)),
        out_shape=jax.ShapeDtypeStruct((rows, n), F32),
        compiler_params=_params(1),
        name="ada",
    )(cond, w_ada, b_ada.reshape(1, n))


def _in_kernel(x_ref, sh_ref, sc_ref, g_ref, tab_ref, w_ref, q_ref, k_ref, v_ref, u_ref, gate_ref):
    h = (_rms(x_ref[...], g_ref[...]) * (1.0 + sc_ref[0]) + sh_ref[0]).astype(BF16)
    cos = tab_ref[:, 0:LANES]
    sin_lo = tab_ref[:, LANES:2 * LANES]
    sin_hi = tab_ref[:, 2 * LANES:3 * LANES]

    def proj(lo, n):
        return jnp.dot(h, w_ref[:, lo:lo + n], preferred_element_type=F32)

    def rope(p):
        return (p * cos + pltpu.roll(p, LANES - ROPE_AXIS_DIM // 2, 1) * sin_lo
                + pltpu.roll(p, ROPE_AXIS_DIM // 2, 1) * sin_hi)

    def rope2(p):
        return jnp.concatenate([rope(p[:, :LANES]), rope(p[:, LANES:])], axis=1)

    for g in range(GROUP):
        q_ref[g] = (rope2(proj(g * KV_WIDTH, KV_WIDTH)) * (Q_SCALE * LOG2E)).astype(BF16)
    k_ref[...] = rope2(proj(K_OFF, KV_WIDTH)).astype(BF16)
    v_ref[...] = proj(V_OFF, KV_WIDTH).astype(BF16)
    u_ref[...] = proj(CONV_OFF, CONV_CH) * _sigmoid(proj(CONV_OFF + CONV_CH, CONV_CH))
    for j in range(4):
        gate_ref[:, j * 512:(j + 1) * 512] = _sigmoid(proj(GATE_OFF + j * 512, 512)).astype(BF16)


def _in_proj(x2, sh, sc, g, tab, w, seq, tm):
    t, d = x2.shape
    return pl.pallas_call(
        _in_kernel,
        grid=(t // tm,),
        in_specs=[pl.BlockSpec((tm, d), lambda i: (i, 0)),
                  pl.BlockSpec((1, 1, d), lambda i: ((i * tm) // seq, 0, 0)),
                  pl.BlockSpec((1, 1, d), lambda i: ((i * tm) // seq, 0, 0)),
                  pl.BlockSpec((1, d), lambda i: (0, 0)),
                  pl.BlockSpec((tm, 3 * LANES), lambda i: (i % (seq // tm), 0)),
                  pl.BlockSpec((d, IN_WIDTH), lambda i: (0, 0))],
        out_specs=[pl.BlockSpec((GROUP, tm, KV_WIDTH), lambda i: (0, i, 0)),
                   pl.BlockSpec((tm, KV_WIDTH), lambda i: (i, 0)),
                   pl.BlockSpec((tm, KV_WIDTH), lambda i: (i, 0)),
                   pl.BlockSpec((tm, CONV_CH), lambda i: (i, 0)),
                   pl.BlockSpec((tm, 2 * d), lambda i: (i, 0))],
        out_shape=[jax.ShapeDtypeStruct((GROUP, t, KV_WIDTH), BF16),
                   jax.ShapeDtypeStruct((t, KV_WIDTH), BF16),
                   jax.ShapeDtypeStruct((t, KV_WIDTH), BF16),
                   jax.ShapeDtypeStruct((t, CONV_CH), F32),
                   jax.ShapeDtypeStruct((t, 2 * d), BF16)],
        compiler_params=_params(1),
        name="in_proj",
    )(x2, sh, sc, g, tab, w)


def _ctx_kernel(x_ref, sh_ref, sc_ref, g_ref, w_ref, k_ref, v_ref):
    h = (_rms(x_ref[...], g_ref[...]) * (1.0 + sc_ref[...]) + sh_ref[...]).astype(BF16)
    k_ref[...] = jnp.dot(h, w_ref[:, :KV_WIDTH], preferred_element_type=F32).astype(BF16)
    v_ref[...] = jnp.dot(h, w_ref[:, KV_WIDTH:], preferred_element_type=F32).astype(BF16)


def _ctx_proj(c2, sh, sc, g, w, tm):
    t, d = c2.shape
    return pl.pallas_call(
        _ctx_kernel,
        grid=(t // tm,),
        in_specs=[pl.BlockSpec((tm, d), lambda i: (i, 0)),
                  pl.BlockSpec((1, d), lambda i: (0, 0)),
                  pl.BlockSpec((1, d), lambda i: (0, 0)),
                  pl.BlockSpec((1, d), lambda i: (0, 0)),
                  pl.BlockSpec((d, 2 * KV_WIDTH), lambda i: (0, 0))],
        out_specs=[pl.BlockSpec((tm, KV_WIDTH), lambda i: (i, 0)),
                   pl.BlockSpec((tm, KV_WIDTH), lambda i: (i, 0))],
        out_shape=[jax.ShapeDtypeStruct((t, KV_WIDTH), BF16),
                   jax.ShapeDtypeStruct((t, KV_WIDTH), BF16)],
        compiler_params=_params(1),
        name="ctx_proj",
    )(c2, sh, sc, g, w)


def _attn_kernel(sink_ref, q_ref, kp_ref, kc_ref, kn_ref, vp_ref, vc_ref, vn_ref, kx_ref, vx_ref,
                 o_ref, *, nb):
    n = pl.program_id(1)
    qs = q_ref[...].reshape(GROUP * BLOCK, KV_WIDTH)
    kw = jnp.concatenate([kp_ref[...], kc_ref[...], kn_ref[...], kx_ref[...]], axis=0)
    vw = jnp.concatenate([vp_ref[...], vc_ref[...], vn_ref[...], vx_ref[...]], axis=0)
    nk = kw.shape[0]
    qi = lax.broadcasted_iota(jnp.int32, (BLOCK, BLOCK), 0)
    kj = lax.broadcasted_iota(jnp.int32, (BLOCK, BLOCK), 1)
    bias_prev = jnp.where((kj >= qi) & (n > 0), 0.0, NEG_INF).astype(F32)
    bias_next = jnp.where((kj <= qi) & (n < nb - 1), 0.0, NEG_INF).astype(F32)
    bias_prev = jnp.concatenate([bias_prev] * GROUP, axis=0)
    bias_next = jnp.concatenate([bias_next] * GROUP, axis=0)
    lane = lax.broadcasted_iota(jnp.int32, (nk, KV_WIDTH), 1)
    out_lane = lax.broadcasted_iota(jnp.int32, (1, KV_WIDTH), 1)
    scores = []
    for h in range(N_KV_HEADS):
        km = jnp.where(lane // HEAD_DIM == h, kw, jnp.zeros_like(kw))
        s = lax.dot_general(qs, km, _NT, preferred_element_type=F32)
        scores.append(jnp.concatenate([s[:, :BLOCK] + bias_prev, s[:, BLOCK:2 * BLOCK],
                                       s[:, 2 * BLOCK:3 * BLOCK] + bias_next, s[:, 3 * BLOCK:]], axis=1))
    acc = jnp.zeros((GROUP * BLOCK, KV_WIDTH), F32)
    for h in range(N_KV_HEADS):
        s = scores[h]
        sink = jnp.concatenate(
            [jnp.full((BLOCK, 1), sink_ref[h * GROUP + g] * LOG2E, F32) for g in range(GROUP)], axis=0)
        m = jnp.maximum(jnp.max(s, axis=-1, keepdims=True), sink)
        p = jnp.exp2(s - m).astype(BF16)
        ones_lane = ((h + 1) % N_KV_HEADS) * HEAD_DIM
        vm = jnp.where(lane // HEAD_DIM == h, vw, jnp.where(lane == ones_lane, 1.0, 0.0).astype(BF16))
        res = jnp.dot(p, vm, preferred_element_type=F32)
        denom = res[:, ones_lane:ones_lane + 1] + jnp.exp2(sink - m)
        acc = jnp.where(out_lane // HEAD_DIM == h, res * (1.0 / denom), acc)
    o_ref[...] = acc.reshape(GROUP, BLOCK, KV_WIDTH).astype(BF16)


def _attention(sink, q, k, v, kx, vx, batch, seq, ctx_len):
    nb = seq // BLOCK
    t = batch * seq

    def cur(b, n):
        return (b * nb + n, 0)

    def prev(b, n):
        return (b * nb + jnp.maximum(n - 1, 0), 0)

    def nxt(b, n):
        return (b * nb + jnp.minimum(n + 1, nb - 1), 0)

    kv = lambda f: pl.BlockSpec((BLOCK, KV_WIDTH), f)
    cx = pl.BlockSpec((ctx_len, KV_WIDTH), lambda b, n: (b, 0))
    qo = pl.BlockSpec((GROUP, BLOCK, KV_WIDTH), lambda b, n: (0, b * nb + n, 0))
    return pl.pallas_call(
        functools.partial(_attn_kernel, nb=nb),
        grid=(batch, nb),
        in_specs=[pl.BlockSpec(memory_space=pltpu.SMEM), qo,
                  kv(prev), kv(cur), kv(nxt), kv(prev), kv(cur), kv(nxt), cx, cx],
        out_specs=qo,
        out_shape=jax.ShapeDtypeStruct((GROUP, t, KV_WIDTH), BF16),
        compiler_params=_params(2),
        name="attn",
    )(sink, q, k, k, k, v, v, v, kx, vx)


def _conv_kernel(up_ref, uc_ref, un_ref, dw_ref, dwb_ref, lng_ref, lnb_ref, o_ref, xp_ref, ph_ref,
                 *, tiles_per_seq, tc, rows):
    i = pl.program_id(0)
    first = (i % tiles_per_seq) == 0
    last = (i % tiles_per_seq) == tiles_per_seq - 1
    xp_ref[0:HALO, :] = jnp.where(first, 0.0, up_ref[...])
    xp_ref[HALO:HALO + tc, :] = uc_ref[...]
    xp_ref[HALO + tc:2 * HALO + tc, :] = jnp.where(last, 0.0, un_ref[...])
    span = ph_ref.shape[1]
    for r in range(SUBLANES):
        ph_ref[r] = xp_ref[r:r + span, :]
    for c in range(tc // rows):
        acc = jnp.zeros((rows, CONV_CH), F32)
        for k in range(CONV_WIDTH):
            q, r = divmod(k + HALO - CONV_PAD, SUBLANES)
            r0 = c * rows + q * SUBLANES
            acc = acc + ph_ref[r, r0:r0 + rows, :] * dw_ref[k:k + 1, :]
        acc = acc + dwb_ref[...]
        mu = jnp.mean(acc, axis=-1, keepdims=True)
        xc = acc - mu
        var = jnp.mean(xc * xc, axis=-1, keepdims=True)
        y = xc * lax.rsqrt(var + NORM_EPS) * lng_ref[...] + lnb_ref[...]
        o_ref[c * rows:(c + 1) * rows, :] = (y * _sigmoid(y)).astype(BF16)


def _conv(u, dw, dwb, lng, lnb, seq, tc, rows=64):
    t, ch = u.shape
    per = tc // HALO
    nh = t // HALO
    return pl.pallas_call(
        functools.partial(_conv_kernel, tiles_per_seq=seq // tc, tc=tc, rows=rows),
        grid=(t // tc,),
        in_specs=[pl.BlockSpec((HALO, ch), lambda i: (jnp.maximum(i * per - 1, 0), 0)),
                  pl.BlockSpec((tc, ch), lambda i: (i, 0)),
                  pl.BlockSpec((HALO, ch), lambda i: (jnp.minimum((i + 1) * per, nh - 1), 0)),
                  pl.BlockSpec((CONV_WIDTH, ch), lambda i: (0, 0)),
                  pl.BlockSpec((1, ch), lambda i: (0, 0)),
                  pl.BlockSpec((1, ch), lambda i: (0, 0)),
                  pl.BlockSpec((1, ch), lambda i: (0, 0))],
        out_specs=pl.BlockSpec((tc, ch), lambda i: (i, 0)),
        out_shape=jax.ShapeDtypeStruct((t, ch), BF16),
        scratch_shapes=[pltpu.VMEM((tc + 2 * HALO, ch), F32),
                        pltpu.VMEM((SUBLANES, tc + 2 * HALO - SUBLANES, ch), F32)],
        compiler_params=_params(1),
        name="conv",
    )(u, u, u, dw, dwb, lng, lnb)


def _merge_kernel(o_ref, hc_ref, gate_ref, x_ref, gt_ref, sh_ref, sc_ref, gpost_ref, gpre_ref,
                  woa_ref, woc_ref, wout_ref, x1_ref, h2t_ref):
    d = x_ref.shape[1]
    y_attn = jnp.dot(o_ref[0], woa_ref[0:KV_WIDTH, :], preferred_element_type=F32)
    for g in range(1, GROUP):
        y_attn = y_attn + jnp.dot(o_ref[g], woa_ref[g * KV_WIDTH:(g + 1) * KV_WIDTH, :],
                                  preferred_element_type=F32)
    y_conv = jnp.dot(hc_ref[...], woc_ref[...], preferred_element_type=F32)
    mix = (gate_ref[:, :d].astype(F32) * y_attn + gate_ref[:, d:].astype(F32) * y_conv).astype(BF16)
    mixed = jnp.dot(mix, wout_ref[...], preferred_element_type=F32)
    x1 = x_ref[...] + gt_ref[0] * _rms(mixed, gpost_ref[...])
    x1_ref[...] = x1
    h2 = _rms(x1, gpre_ref[...]) * (1.0 + sc_ref[0]) + sh_ref[0]
    h2t_ref[...] = h2.T.astype(BF16)


def _merge(o, hc, gates, x2, gt1, sh2, sc2, gpost, gpre, woa, woc, wout, seq, tm):
    t, d = x2.shape
    mod = pl.BlockSpec((1, 1, d), lambda i: ((i * tm) // seq, 0, 0))
    vec = pl.BlockSpec((1, d), lambda i: (0, 0))
    full = lambda a: pl.BlockSpec(a.shape, lambda i: (0, 0))
    return pl.pallas_call(
        _merge_kernel,
        grid=(t // tm,),
        in_specs=[pl.BlockSpec((GROUP, tm, KV_WIDTH), lambda i: (0, i, 0)),
                  pl.BlockSpec((tm, CONV_CH), lambda i: (i, 0)),
                  pl.BlockSpec((tm, 2 * d), lambda i: (i, 0)),
                  pl.BlockSpec((tm, d), lambda i: (i, 0)),
                  mod, mod, mod, vec, vec, full(woa), full(woc), full(wout)],
        out_specs=[pl.BlockSpec((tm, d), lambda i: (i, 0)),
                   pl.BlockSpec((d, tm), lambda i: (0, i))],
        out_shape=[jax.ShapeDtypeStruct((t, d), F32),
                   jax.ShapeDtypeStruct((d, t), BF16)],
        compiler_params=_params(1),
        name="merge",
    )(o, hc, gates, x2, gt1, sh2, sc2, gpost, gpre, woa, woc, wout)


def _top16(s, want_rank):
    rem = s
    rank = jnp.full(s.shape, float(PEER_TOPK), F32) if want_rank else None
    vals = []
    for j in range(PEER_TOPK):
        m = jnp.max(rem, axis=0, keepdims=True)
        eq = rem == m
        if want_rank:
            rank = jnp.where(eq, float(j), rank)
        rem = jnp.where(eq, NEG_BIG, rem)
        vals.append(m)
    return jnp.concatenate(vals, axis=0), rank


def _kth_largest(c, k):
    rem = c
    cnt = jnp.zeros((1, c.shape[1]), F32)
    thr = jnp.full((1, c.shape[1]), NEG_BIG, F32)
    for _ in range(k):
        m = jnp.max(rem, axis=0, keepdims=True)
        eq = rem == m
        thr = jnp.where(cnt < float(k), m, thr)
        cnt = cnt + jnp.sum(jnp.where(eq, 1.0, 0.0), axis=0, keepdims=True)
        rem = jnp.where(eq, NEG_BIG, rem)
    return thr


def _retrieve(s1, s2):
    v1, _ = _top16(s1, False)
    v2, rank2 = _top16(s2, True)
    half = PEER_TOPK // 2
    cand = jnp.concatenate([v1[0:1] + v2]
                           + [v1[i:i + 1] + v2[0:half] for i in range(1, half)]
                           + [v1[half:] + v2[0:1]], axis=0)
    thr = _kth_largest(cand, PEER_TOPK)
    ev1 = jnp.exp(v1 - v1[0:1])
    ev2 = jnp.exp(v2 - v2[0:1])
    z = jnp.zeros_like(thr)
    n = jnp.zeros(s1.shape, F32)
    for i in range(PEER_TOPK):
        sel = (v1[i:i + 1] + v2) >= thr
        cnt = jnp.sum(jnp.where(sel, 1.0, 0.0), axis=0, keepdims=True)
        z = z + ev1[i:i + 1] * jnp.sum(jnp.where(sel, ev2, 0.0), axis=0, keepdims=True)
        n = jnp.where(s1 == v1[i:i + 1], cnt, n)
    e1 = jnp.where(s1 >= v1[PEER_TOPK - 1:], jnp.exp(s1 - v1[0:1]) * (0.5 / z), 0.0)
    return n, e1, rank2.astype(BF16), jnp.exp(s2 - v2[0:1]).astype(BF16)


def _score_kernel(h2t_ref, wqt_ref, keys_ref, n_ref, e1_ref, r2_ref, e2_ref):
    qpt = jnp.dot(wqt_ref[...], h2t_ref[...], preferred_element_type=F32)
    for h in range(PEER_HEADS):
        q1 = qpt[h * PEER_QDIM:h * PEER_QDIM + PEER_HALF].astype(BF16)
        q2 = qpt[h * PEER_QDIM + PEER_HALF:(h + 1) * PEER_QDIM].astype(BF16)
        s1 = jnp.dot(keys_ref[h, 0], q1, preferred_element_type=F32)
        s2 = jnp.dot(keys_ref[h, 1], q2, preferred_element_type=F32)
        for l0 in range(0, s1.shape[1], LANES):
            ls = slice(l0, l0 + LANES)
            n_ref[h, :, ls], e1_ref[h, :, ls], r2_ref[h, :, ls], e2_ref[h, :, ls] = _retrieve(s1[:, ls], s2[:, ls])


def _score(h2t, wqt, keys, tt):
    d, t = h2t.shape
    blk = pl.BlockSpec((PEER_HEADS, N_KEYS, tt), lambda j: (0, 0, j))
    shp = lambda dt: jax.ShapeDtypeStruct((PEER_HEADS, N_KEYS, t), dt)
    return pl.pallas_call(
        _score_kernel,
        grid=(t // tt,),
        in_specs=[pl.BlockSpec((d, tt), lambda j: (0, j)),
                  pl.BlockSpec(wqt.shape, lambda j: (0, 0)),
                  pl.BlockSpec(keys.shape, lambda j: (0, 0, 0, 0))],
        out_specs=[blk, blk, blk, blk],
        out_shape=[shp(F32), shp(F32), shp(BF16), shp(BF16)],
        compiler_params=_params(1),
        name="peer_score",
    )(h2t, wqt, keys)


def _expert_kernel(h2t_ref, u_ref, vt_ref, n_ref, e1_ref, r2_ref, e2_ref, x1_ref, gt_ref, gpost_ref,
                   o_ref, acc_ref, at_ref, ga_ref, *, eb, ec):
    i = pl.program_id(1)

    @pl.when(i == 0)
    def _():
        acc_ref[...] = jnp.zeros_like(acc_ref)

    tt = h2t_ref.shape[1]
    per = ec // N_KEYS
    n_chunks = eb // ec
    pk = N_KEYS // BF16_ROWS

    def up(c, ls):
        at = jnp.dot(u_ref[c * ec:(c + 1) * ec, :], h2t_ref[:, ls], preferred_element_type=F32)
        at_ref[c % 2, :, :, ls] = at.reshape(ec // BF16_ROWS, BF16_ROWS, GATE_LANES)

    def down(c, ls):
        p = c // DOWN_CHUNKS
        acc_ref[:, ls] += jnp.dot(vt_ref[:, p * DOWN_CHUNKS * ec:(p + 1) * DOWN_CHUNKS * ec],
                                  ga_ref[p % 2, :, :, ls].reshape(DOWN_CHUNKS * ec, GATE_LANES),
                                  preferred_element_type=F32)

    def gate(c, aa, ls):
        al = c * per + aa
        rows = slice(aa * pk, (aa + 1) * pk)
        ga_rows = slice(((c % DOWN_CHUNKS) * per + aa) * pk, ((c % DOWN_CHUNKS) * per + aa + 1) * pk)
        g = None
        for h in range(PEER_HEADS):
            nrow = jnp.broadcast_to(n_ref[h, al:al + 1, ls], (BF16_ROWS, GATE_LANES)).astype(BF16)
            erow = jnp.broadcast_to(e1_ref[h, al:al + 1, ls], (BF16_ROWS, GATE_LANES)).astype(BF16)
            term = jnp.where(r2_ref[h, :, :, ls] < nrow[None], e2_ref[h, :, :, ls] * erow[None], 0.0)
            g = term if g is None else g + term
        at = at_ref[c % 2, rows, :, ls]
        act = (at * (1.0 + lax.erf(at * SQRT_HALF))).astype(BF16)
        ga_ref[(c // DOWN_CHUNKS) % 2, ga_rows, :, ls] = g * act

    slabs = [slice(l0, l0 + GATE_LANES) for l0 in range(0, tt, GATE_LANES)]
    for ls in slabs:
        up(0, ls)
    for c in range(n_chunks):
        for ls in slabs:
            gate(c, 0, ls)
            if c + 1 < n_chunks:
                up(c + 1, ls)
            for aa in range(1, per):
                gate(c, aa, ls)
            if c % DOWN_CHUNKS == DOWN_CHUNKS - 1:
                down(c, ls)

    @pl.when(i == pl.num_programs(1) - 1)
    def _():
        y = acc_ref[...].T
        o_ref[...] = x1_ref[...] + gt_ref[0] * _rms(y, gpost_ref[...])


def _experts(h2t, u, vt, n, e1, r2, e2, x1, gt2, gpost, seq, tt, eb, ec=256):
    d, t = h2t.shape
    ne = u.shape[0]
    a = eb // N_KEYS
    pk = N_KEYS // BF16_ROWS
    sub = pl.BlockSpec((PEER_HEADS, a, tt), lambda j, i: (0, i, j))
    allk = pl.BlockSpec((PEER_HEADS, pk, BF16_ROWS, tt), lambda j, i: (0, 0, 0, j))
    tiled = lambda v: v.reshape(PEER_HEADS, pk, BF16_ROWS, t)
    return pl.pallas_call(
        functools.partial(_expert_kernel, eb=eb, ec=ec),
        grid=(t // tt, ne // eb),
        in_specs=[pl.BlockSpec((d, tt), lambda j, i: (0, j)),
                  pl.BlockSpec((eb, d), lambda j, i: (i, 0)),
                  pl.BlockSpec((d, eb), lambda j, i: (0, i)),
                  sub, sub, allk, allk,
                  pl.BlockSpec((tt, d), lambda j, i: (j, 0)),
                  pl.BlockSpec((1, 1, d), lambda j, i: ((j * tt) // seq, 0, 0)),
                  pl.BlockSpec((1, d), lambda j, i: (0, 0))],
        out_specs=pl.BlockSpec((tt, d), lambda j, i: (j, 0)),
        out_shape=jax.ShapeDtypeStruct((t, d), F32),
        scratch_shapes=[pltpu.VMEM((d, tt), F32),
                        pltpu.VMEM((2, ec // BF16_ROWS, BF16_ROWS, tt), F32),
                        pltpu.VMEM((2, DOWN_CHUNKS * ec // BF16_ROWS, BF16_ROWS, tt), BF16)],
        compiler_params=_params(2),
        name="peer_experts",
    )(h2t, u, vt, n, e1, tiled(r2), tiled(e2), x1, gt2, gpost)


def _rope_table(seq):
    t = jnp.arange(seq, dtype=jnp.int32)
    row = (t // GRID_W).astype(F32)
    col = (t % GRID_W).astype(F32)
    freqs = ROPE_BASE ** (-jnp.arange(0, ROPE_AXIS_DIM, 2, dtype=F32) / ROPE_AXIS_DIM)
    ang_r = row[:, None] * freqs[None, :]
    ang_c = col[:, None] * freqs[None, :]
    zero = jnp.zeros_like(ang_r)
    cos = jnp.concatenate([jnp.cos(ang_r), jnp.cos(ang_r), jnp.cos(ang_c), jnp.cos(ang_c)], axis=1)
    s_lo = jnp.concatenate([-jnp.sin(ang_r), zero, -jnp.sin(ang_c), zero], axis=1)
    s_hi = jnp.concatenate([zero, jnp.sin(ang_r), zero, jnp.sin(ang_c)], axis=1)
    two = lambda a: jnp.concatenate([a, a], axis=1)
    return jnp.concatenate([two(cos), two(s_lo), two(s_hi)], axis=1)


def _layer(x2, ctx2, cond, rope_tab, batch, seq, ctx_len,
           w_ada, b_ada, g_pre_mix, g_post_mix, g_pre_ffn, g_post_ffn, w_in, attn_sink, w_o_attn,
           conv_dw, conv_dw_b, conv_ln_g, conv_ln_b, w_o_conv, w_out, peer_wq, peer_keys, peer_u, peer_v):
    d = D_MODEL
    t = batch * seq
    tm = min(512, seq)
    row = lambda v: v.reshape(1, -1)

    mod = _ada(cond, w_ada, b_ada)
    per_batch = [mod[:batch, j * d:(j + 1) * d].reshape(batch, 1, d) for j in range(6)]
    sh1, sc1, gt1, sh2, sc2, gt2 = per_batch
    csh1 = mod[batch:batch + 1, 0:d]
    csc1 = mod[batch:batch + 1, d:2 * d]

    wq = w_in[:, :K_OFF].reshape(d, N_KV_HEADS, GROUP, HEAD_DIM).transpose(0, 2, 1, 3).reshape(d, ATTN_WIDTH)
    w_in_b = jnp.concatenate([wq, w_in[:, K_OFF:]], axis=1).astype(BF16)
    woa = w_o_attn.reshape(N_KV_HEADS, GROUP, HEAD_DIM, d).transpose(1, 0, 2, 3).reshape(ATTN_WIDTH, d).astype(BF16)

    q, k, v, u, gates = _in_proj(x2, sh1, sc1, row(g_pre_mix), rope_tab, w_in_b, seq, tm)
    kx, vx = _ctx_proj(ctx2, csh1, csc1, row(g_pre_mix), w_in_b[:, K_OFF:CONV_OFF], min(512, ctx2.shape[0]))
    o = _attention(attn_sink, q, k, v, kx, vx, batch, seq, ctx_len)
    hc = _conv(u, conv_dw, row(conv_dw_b), row(conv_ln_g), row(conv_ln_b), seq, min(256, seq))
    x1, h2t = _merge(o, hc, gates, x2, gt1, sh2, sc2, row(g_post_mix), row(g_pre_ffn),
                     woa, w_o_conv.astype(BF16), w_out.astype(BF16), seq, tm)
    n, e1, r2, e2 = _score(h2t, peer_wq.T.astype(BF16), peer_keys.astype(BF16), min(256, seq))
    return _experts(h2t, peer_u.astype(BF16), peer_v.T.astype(BF16), n, e1, r2, e2, x1, gt2,
                    row(g_post_ffn), seq, min(512, seq), 2048)


def kernel(x, c, ctx, c_ctx, w_ada, b_ada, g_pre_mix, g_post_mix, g_pre_ffn, g_post_ffn, w_in, attn_sink,
           w_o_attn, conv_dw, conv_dw_b, conv_ln_g, conv_ln_b, w_o_conv, w_out, peer_wq, peer_keys, peer_u,
           peer_v):
    batch, seq, d = x.shape
    ctx_len = ctx.shape[1]
    depth = w_ada.shape[0]
    assert depth == 1, "context-stream update between layers is not implemented"
    assert d == D_MODEL and seq % BLOCK == 0 and seq % GRID_W == 0 and batch < 16
    cond = jnp.zeros((16, d), F32).at[:batch].set(c).at[batch].set(c_ctx)
    rope_tab = _rope_table(seq)
    out = _layer(x.reshape(batch * seq, d), ctx.reshape(batch * ctx_len, d), cond, rope_tab, batch, seq, ctx_len,
                 w_ada[0], b_ada[0], g_pre_mix[0], g_post_mix[0], g_pre_ffn[0], g_post_ffn[0], w_in[0],
                 attn_sink[0], w_o_attn[0], conv_dw[0], conv_dw_b[0], conv_ln_g[0], conv_ln_b[0], w_o_conv[0],
                 w_out[0], peer_wq[0], peer_keys[0], peer_u[0], peer_v[0])
    return out.reshape(batch, seq, d)
```

```python
import functools

import jax
import jax.numpy as jnp
from jax import lax
from jax.experimental import pallas as pl
from jax.experimental.pallas import tpu as pltpu

F32 = jnp.float32
BF16 = jnp.bfloat16

D_MODEL = 1024
GRID_W = 64
N_HEADS = 16
N_KV_HEADS = 4
GROUP = N_HEADS // N_KV_HEADS
HEAD_DIM = 64
ATTN_WIDTH = N_HEADS * HEAD_DIM
KV_WIDTH = N_KV_HEADS * HEAD_DIM
WINDOW = 128
BLOCK = 128
ROPE_AXIS_DIM = HEAD_DIM // 2
ROPE_BASE = 10000.0
CONV_CH = D_MODEL // 2
CONV_WIDTH = 31
CONV_PAD = CONV_WIDTH // 2
K_OFF = ATTN_WIDTH
V_OFF = K_OFF + KV_WIDTH
CONV_OFF = V_OFF + KV_WIDTH
GATE_OFF = CONV_OFF + 2 * CONV_CH
IN_WIDTH = GATE_OFF + 2 * D_MODEL
PEER_HEADS = 8
N_KEYS = 128
N_EXPERTS = N_KEYS * N_KEYS
PEER_QDIM = 256
PEER_HALF = PEER_QDIM // 2
PEER_TOPK = 16
NORM_EPS = 1e-6
NEG_INF = -1e30
NEG_BIG = -3.0e38
SQRT_HALF = 0.7071067811865476
Q_SCALE = HEAD_DIM ** -0.5
LOG2E = 1.4426950408889634

LANES = 128
ATTN_Q = 2 * BLOCK
SUBLANES = 8
BF16_ROWS = 16
HALO = 16
GATE_LANES = 256
DOWN_CHUNKS = 2
VMEM_LIMIT = 56 * 1024 * 1024

_NT = (((1,), (1,)), ((), ()))
_TN = (((0,), (0,)), ((), ()))


def _params(n_axes, vmem=VMEM_LIMIT, flags=None):
    return pltpu.CompilerParams(dimension_semantics=("arbitrary",) * n_axes, vmem_limit_bytes=vmem, flags=flags)


def _rms(x, g):
    return x * lax.rsqrt(jnp.mean(x * x, axis=-1, keepdims=True) + NORM_EPS) * g


def _sigmoid(x):
    return jax.nn.sigmoid(x)


def _ada_kernel(c_ref, w_ref, b_ref, o_ref):
    c = c_ref[...]
    s = (c * _sigmoid(c)).astype(BF16)
    o_ref[...] = jnp.dot(s, w_ref[...].astype(BF16), preferred_element_type=F32) + b_ref[...]


def _ada(cond, w_ada, b_ada):
    rows, d = cond.shape
    n = w_ada.shape[1]
    bn = 1536
    return pl.pallas_call(
        _ada_kernel,
        grid=(n // bn,),
        in_specs=[pl.BlockSpec((rows, d), lambda j: (0, 0)),
                  pl.BlockSpec((d, bn), lambda j: (0, j)),
                  pl.BlockSpec((1, bn), lambda j: (0, j))],
        out_specs=pl.BlockSpec((rows, bn), lambda j: (0, j)),
        out_shape=jax.ShapeDtypeStruct((rows, n), F32),
        compiler_params=_params(1),
        name="ada",
    )(cond, w_ada, b_ada.reshape(1, n))


def _in_kernel(x_ref, sh_ref, sc_ref, g_ref, tab_ref, w_ref, q_ref, k_ref, v_ref, u_ref, gate_ref):
    h = (_rms(x_ref[...], g_ref[...]) * (1.0 + sc_ref[0]) + sh_ref[0]).astype(BF16)
    cos = tab_ref[:, 0:LANES]
    sin_lo = tab_ref[:, LANES:2 * LANES]
    sin_hi = tab_ref[:, 2 * LANES:3 * LANES]

    def proj(lo, n):
        return jnp.dot(h, w_ref[:, lo:lo + n], preferred_element_type=F32)

    def rope(p):
        return (p * cos + pltpu.roll(p, LANES - ROPE_AXIS_DIM // 2, 1) * sin_lo
                + pltpu.roll(p, ROPE_AXIS_DIM // 2, 1) * sin_hi)

    def rope2(p):
        return jnp.concatenate([rope(p[:, :LANES]), rope(p[:, LANES:])], axis=1)

    for g in range(GROUP):
        q_ref[g] = (rope2(proj(g * KV_WIDTH, KV_WIDTH)) * (Q_SCALE * LOG2E)).astype(BF16)
    k_ref[...] = rope2(proj(K_OFF, KV_WIDTH)).astype(BF16)
    v_ref[...] = proj(V_OFF, KV_WIDTH).astype(BF16)
    u_ref[...] = proj(CONV_OFF, CONV_CH) * _sigmoid(proj(CONV_OFF + CONV_CH, CONV_CH))
    for j in range(4):
        gate_ref[:, j * 512:(j + 1) * 512] = _sigmoid(proj(GATE_OFF + j * 512, 512)).astype(BF16)


def _in_proj(x2, sh, sc, g, tab, w, seq, tm):
    t, d = x2.shape
    return pl.pallas_call(
        _in_kernel,
        grid=(t // tm,),
        in_specs=[pl.BlockSpec((tm, d), lambda i: (i, 0)),
                  pl.BlockSpec((1, 1, d), lambda i: ((i * tm) // seq, 0, 0)),
                  pl.BlockSpec((1, 1, d), lambda i: ((i * tm) // seq, 0, 0)),
                  pl.BlockSpec((1, d), lambda i: (0, 0)),
                  pl.BlockSpec((tm, 3 * LANES), lambda i: (i % (seq // tm), 0)),
                  pl.BlockSpec((d, IN_WIDTH), lambda i: (0, 0))],
        out_specs=[pl.BlockSpec((GROUP, tm, KV_WIDTH), lambda i: (0, i, 0)),
                   pl.BlockSpec((tm, KV_WIDTH), lambda i: (i, 0)),
                   pl.BlockSpec((tm, KV_WIDTH), lambda i: (i, 0)),
                   pl.BlockSpec((tm, CONV_CH), lambda i: (i, 0)),
                   pl.BlockSpec((tm, 2 * d), lambda i: (i, 0))],
        out_shape=[jax.ShapeDtypeStruct((GROUP, t, KV_WIDTH), BF16),
                   jax.ShapeDtypeStruct((t, KV_WIDTH), BF16),
                   jax.ShapeDtypeStruct((t, KV_WIDTH), BF16),
                   jax.ShapeDtypeStruct((t, CONV_CH), F32),
                   jax.ShapeDtypeStruct((t, 2 * d), BF16)],
        compiler_params=_params(1),
        name="in_proj",
    )(x2, sh, sc, g, tab, w)


def _ctx_kernel(x_ref, sh_ref, sc_ref, g_ref, w_ref, k_ref, v_ref):
    h = (_rms(x_ref[...], g_ref[...]) * (1.0 + sc_ref[...]) + sh_ref[...]).astype(BF16)
    k_ref[...] = jnp.dot(h, w_ref[:, :KV_WIDTH], preferred_element_type=F32).astype(BF16)
    v_ref[...] = jnp.dot(h, w_ref[:, KV_WIDTH:], preferred_element_type=F32).astype(BF16)


def _ctx_proj(c2, sh, sc, g, w, tm):
    t, d = c2.shape
    return pl.pallas_call(
        _ctx_kernel,
        grid=(t // tm,),
        in_specs=[pl.BlockSpec((tm, d), lambda i: (i, 0)),
                  pl.BlockSpec((1, d), lambda i: (0, 0)),
                  pl.BlockSpec((1, d), lambda i: (0, 0)),
                  pl.BlockSpec((1, d), lambda i: (0, 0)),
                  pl.BlockSpec((d, 2 * KV_WIDTH), lambda i: (0, 0))],
        out_specs=[pl.BlockSpec((tm, KV_WIDTH), lambda i: (i, 0)),
                   pl.BlockSpec((tm, KV_WIDTH), lambda i: (i, 0))],
        out_shape=[jax.ShapeDtypeStruct((t, KV_WIDTH), BF16),
                   jax.ShapeDtypeStruct((t, KV_WIDTH), BF16)],
        compiler_params=_params(1),
        name="ctx_proj",
    )(c2, sh, sc, g, w)


def _attn_kernel(sink_ref, q_ref, kp_ref, kc_ref, kn_ref, vp_ref, vc_ref, vn_ref, kx_ref, vx_ref,
                 o_ref, *, n_steps):
    n = pl.program_id(1)
    qs = q_ref[...].reshape(GROUP * ATTN_Q, KV_WIDTH)
    kw = jnp.concatenate([kp_ref[...], kc_ref[...], kn_ref[...], kx_ref[...]], axis=0)
    vw = jnp.concatenate([vp_ref[...], vc_ref[...], vn_ref[...], vx_ref[...]], axis=0)
    nk = kw.shape[0]
    nwin = ATTN_Q + 2 * BLOCK
    kj = lax.broadcasted_iota(jnp.int32, (nwin, ATTN_Q), 0)
    qi = lax.broadcasted_iota(jnp.int32, (nwin, ATTN_Q), 1)
    ok = (kj >= qi) & (kj <= qi + 2 * WINDOW)
    ok = ok & ((kj >= BLOCK) | (n > 0)) & ((kj < nwin - BLOCK) | (n < n_steps - 1))
    bias = jnp.concatenate([jnp.where(ok, 0.0, NEG_INF).astype(F32)] * GROUP, axis=1)
    lane = lax.broadcasted_iota(jnp.int32, (nk, KV_WIDTH), 1)
    scores = []
    for h in range(N_KV_HEADS):
        km = jnp.where(lane // HEAD_DIM == h, kw, jnp.zeros_like(kw))
        st = lax.dot_general(km, qs, _NT, preferred_element_type=F32)
        scores.append(jnp.concatenate([st[:nwin] + bias, st[nwin:]], axis=0))
    vt = vw.astype(F32).T.astype(BF16)
    ones = jnp.ones((BF16_ROWS, nk), BF16)
    heads = []
    for h in range(N_KV_HEADS):
        st = scores[h]
        sink = jnp.concatenate(
            [jnp.full((1, ATTN_Q), sink_ref[h * GROUP + g] * LOG2E, F32) for g in range(GROUP)], axis=1)
        m = jnp.maximum(jnp.max(st, axis=0, keepdims=True), sink)
        pt = jnp.exp2(st - m).astype(BF16)
        lhs = jnp.concatenate([vt[h * HEAD_DIM:(h + 1) * HEAD_DIM], ones], axis=0)
        rest = jnp.dot(lhs, pt, preferred_element_type=F32)
        denom = rest[HEAD_DIM:HEAD_DIM + 1] + jnp.exp2(sink - m)
        heads.append(rest[:HEAD_DIM] * (1.0 / denom))
    out = jnp.concatenate(heads, axis=0).T
    o_ref[...] = out.reshape(GROUP, ATTN_Q, KV_WIDTH).astype(BF16)


def _attention(sink, q, k, v, kx, vx, batch, seq, ctx_len):
    nb = seq // BLOCK
    per = ATTN_Q // BLOCK
    steps = seq // ATTN_Q
    t = batch * seq

    def cur(b, n):
        return (b * steps + n, 0)

    def prev(b, n):
        return (b * nb + jnp.maximum(n * per - 1, 0), 0)

    def nxt(b, n):
        return (b * nb + jnp.minimum((n + 1) * per, nb - 1), 0)

    kv = lambda f: pl.BlockSpec((BLOCK, KV_WIDTH), f)
    kc = pl.BlockSpec((ATTN_Q, KV_WIDTH), cur)
    cx = pl.BlockSpec((ctx_len, KV_WIDTH), lambda b, n: (b, 0))
    qo = pl.BlockSpec((GROUP, ATTN_Q, KV_WIDTH), lambda b, n: (0, b * steps + n, 0))
    return pl.pallas_call(
        functools.partial(_attn_kernel, n_steps=steps),
        grid=(batch, steps),
        in_specs=[pl.BlockSpec(memory_space=pltpu.SMEM), qo,
                  kv(prev), kc, kv(nxt), kv(prev), kc, kv(nxt), cx, cx],
        out_specs=qo,
        out_shape=jax.ShapeDtypeStruct((GROUP, t, KV_WIDTH), BF16),
        compiler_params=_params(2),
        name="attn",
    )(sink, q, k, k, k, v, v, v, kx, vx)


def _conv_kernel(up_ref, uc_ref, un_ref, dw_ref, dwb_ref, lng_ref, lnb_ref, o_ref, xp_ref, ph_ref,
                 *, tiles_per_seq, tc, rows):
    i = pl.program_id(0)
    first = (i % tiles_per_seq) == 0
    last = (i % tiles_per_seq) == tiles_per_seq - 1
    xp_ref[0:HALO, :] = jnp.where(first, 0.0, up_ref[...])
    xp_ref[HALO:HALO + tc, :] = uc_ref[...]
    xp_ref[HALO + tc:2 * HALO + tc, :] = jnp.where(last, 0.0, un_ref[...])
    span = ph_ref.shape[1]
    for r in range(SUBLANES):
        ph_ref[r] = xp_ref[r:r + span, :]
    for c in range(tc // rows):
        acc = jnp.zeros((rows, CONV_CH), F32)
        for k in range(CONV_WIDTH):
            q, r = divmod(k + HALO - CONV_PAD, SUBLANES)
            r0 = c * rows + q * SUBLANES
            acc = acc + ph_ref[r, r0:r0 + rows, :] * dw_ref[k:k + 1, :]
        acc = acc + dwb_ref[...]
        mu = jnp.mean(acc, axis=-1, keepdims=True)
        xc = acc - mu
        var = jnp.mean(xc * xc, axis=-1, keepdims=True)
        y = xc * lax.rsqrt(var + NORM_EPS) * lng_ref[...] + lnb_ref[...]
        o_ref[c * rows:(c + 1) * rows, :] = (y * _sigmoid(y)).astype(BF16)


def _conv(u, dw, dwb, lng, lnb, seq, tc, rows=64):
    t, ch = u.shape
    per = tc // HALO
    nh = t // HALO
    return pl.pallas_call(
        functools.partial(_conv_kernel, tiles_per_seq=seq // tc, tc=tc, rows=rows),
        grid=(t // tc,),
        in_specs=[pl.BlockSpec((HALO, ch), lambda i: (jnp.maximum(i * per - 1, 0), 0)),
                  pl.BlockSpec((tc, ch), lambda i: (i, 0)),
                  pl.BlockSpec((HALO, ch), lambda i: (jnp.minimum((i + 1) * per, nh - 1), 0)),
                  pl.BlockSpec((CONV_WIDTH, ch), lambda i: (0, 0)),
                  pl.BlockSpec((1, ch), lambda i: (0, 0)),
                  pl.BlockSpec((1, ch), lambda i: (0, 0)),
                  pl.BlockSpec((1, ch), lambda i: (0, 0))],
        out_specs=pl.BlockSpec((tc, ch), lambda i: (i, 0)),
        out_shape=jax.ShapeDtypeStruct((t, ch), BF16),
        scratch_shapes=[pltpu.VMEM((tc + 2 * HALO, ch), F32),
                        pltpu.VMEM((SUBLANES, tc + 2 * HALO - SUBLANES, ch), F32)],
        compiler_params=_params(1),
        name="conv",
    )(u, u, u, dw, dwb, lng, lnb)


def _merge_kernel(o_ref, hc_ref, gate_ref, x_ref, gt_ref, sh_ref, sc_ref, gpost_ref, gpre_ref,
                  woa_ref, woc_ref, wout_ref, x1_ref, h2t_ref):
    d = x_ref.shape[1]
    y_attn = jnp.dot(o_ref[0], woa_ref[0:KV_WIDTH, :], preferred_element_type=F32)
    for g in range(1, GROUP):
        y_attn = y_attn + jnp.dot(o_ref[g], woa_ref[g * KV_WIDTH:(g + 1) * KV_WIDTH, :],
                                  preferred_element_type=F32)
    y_conv = jnp.dot(hc_ref[...], woc_ref[...], preferred_element_type=F32)
    mix = (gate_ref[:, :d].astype(F32) * y_attn + gate_ref[:, d:].astype(F32) * y_conv).astype(BF16)
    mixed = jnp.dot(mix, wout_ref[...], preferred_element_type=F32)
    x1 = x_ref[...] + gt_ref[0] * _rms(mixed, gpost_ref[...])
    x1_ref[...] = x1
    h2 = _rms(x1, gpre_ref[...]) * (1.0 + sc_ref[0]) + sh_ref[0]
    h2t_ref[...] = h2.T.astype(BF16)


def _merge(o, hc, gates, x2, gt1, sh2, sc2, gpost, gpre, woa, woc, wout, seq, tm):
    t, d = x2.shape
    mod = pl.BlockSpec((1, 1, d), lambda i: ((i * tm) // seq, 0, 0))
    vec = pl.BlockSpec((1, d), lambda i: (0, 0))
    full = lambda a: pl.BlockSpec(a.shape, lambda i: (0, 0))
    return pl.pallas_call(
        _merge_kernel,
        grid=(t // tm,),
        in_specs=[pl.BlockSpec((GROUP, tm, KV_WIDTH), lambda i: (0, i, 0)),
                  pl.BlockSpec((tm, CONV_CH), lambda i: (i, 0)),
                  pl.BlockSpec((tm, 2 * d), lambda i: (i, 0)),
                  pl.BlockSpec((tm, d), lambda i: (i, 0)),
                  mod, mod, mod, vec, vec, full(woa), full(woc), full(wout)],
        out_specs=[pl.BlockSpec((tm, d), lambda i: (i, 0)),
                   pl.BlockSpec((d, tm), lambda i: (0, i))],
        out_shape=[jax.ShapeDtypeStruct((t, d), F32),
                   jax.ShapeDtypeStruct((d, t), BF16)],
        compiler_params=_params(1),
        name="merge",
    )(o, hc, gates, x2, gt1, sh2, sc2, gpost, gpre, woa, woc, wout)


def _top16(s, want_rank):
    rem = s
    rank = jnp.full(s.shape, float(PEER_TOPK), F32) if want_rank else None
    vals = []
    for j in range(PEER_TOPK):
        m = jnp.max(rem, axis=0, keepdims=True)
        eq = rem == m
        if want_rank:
            rank = jnp.where(eq, float(j), rank)
        rem = jnp.where(eq, NEG_BIG, rem)
        vals.append(m)
    return jnp.concatenate(vals, axis=0), rank


def _kth_largest(c, k):
    rem = c
    cnt = jnp.zeros((1, c.shape[1]), F32)
    thr = jnp.full((1, c.shape[1]), NEG_BIG, F32)
    for _ in range(k):
        m = jnp.max(rem, axis=0, keepdims=True)
        eq = rem == m
        thr = jnp.where(cnt < float(k), m, thr)
        cnt = cnt + jnp.sum(jnp.where(eq, 1.0, 0.0), axis=0, keepdims=True)
        rem = jnp.where(eq, NEG_BIG, rem)
    return thr


def _retrieve(s1, s2):
    v1, _ = _top16(s1, False)
    v2, rank2 = _top16(s2, True)
    half = PEER_TOPK // 2
    cand = jnp.concatenate([v1[0:1] + v2]
                           + [v1[i:i + 1] + v2[0:half] for i in range(1, half)]
                           + [v1[half:] + v2[0:1]], axis=0)
    thr = _kth_largest(cand, PEER_TOPK)
    ev1 = jnp.exp(v1 - v1[0:1])
    ev2 = jnp.exp(v2 - v2[0:1])
    z = jnp.zeros_like(thr)
    n = jnp.zeros(s1.shape, F32)
    for i in range(PEER_TOPK):
        sel = (v1[i:i + 1] + v2) >= thr
        cnt = jnp.sum(jnp.where(sel, 1.0, 0.0), axis=0, keepdims=True)
        z = z + ev1[i:i + 1] * jnp.sum(jnp.where(sel, ev2, 0.0), axis=0, keepdims=True)
        n = jnp.where(s1 == v1[i:i + 1], cnt, n)
    e1 = jnp.where(s1 >= v1[PEER_TOPK - 1:], jnp.exp(s1 - v1[0:1]) * (0.5 / z), 0.0)
    return n, e1, rank2.astype(BF16), jnp.exp(s2 - v2[0:1]).astype(BF16)


def _score_kernel(h2t_ref, wqt_ref, keys_ref, n_ref, e1_ref, r2_ref, e2_ref):
    qpt = jnp.dot(wqt_ref[...], h2t_ref[...], preferred_element_type=F32)
    for h in range(PEER_HEADS):
        q1 = qpt[h * PEER_QDIM:h * PEER_QDIM + PEER_HALF].astype(BF16)
        q2 = qpt[h * PEER_QDIM + PEER_HALF:(h + 1) * PEER_QDIM].astype(BF16)
        s1 = jnp.dot(keys_ref[h, 0], q1, preferred_element_type=F32)
        s2 = jnp.dot(keys_ref[h, 1], q2, preferred_element_type=F32)
        for l0 in range(0, s1.shape[1], LANES):
            ls = slice(l0, l0 + LANES)
            n_ref[h, :, ls], e1_ref[h, :, ls], r2_ref[h, :, ls], e2_ref[h, :, ls] = _retrieve(s1[:, ls], s2[:, ls])


def _score(h2t, wqt, keys, tt):
    d, t = h2t.shape
    blk = pl.BlockSpec((PEER_HEADS, N_KEYS, tt), lambda j: (0, 0, j))
    shp = lambda dt: jax.ShapeDtypeStruct((PEER_HEADS, N_KEYS, t), dt)
    return pl.pallas_call(
        _score_kernel,
        grid=(t // tt,),
        in_specs=[pl.BlockSpec((d, tt), lambda j: (0, j)),
                  pl.BlockSpec(wqt.shape, lambda j: (0, 0)),
                  pl.BlockSpec(keys.shape, lambda j: (0, 0, 0, 0))],
        out_specs=[blk, blk, blk, blk],
        out_shape=[shp(F32), shp(F32), shp(BF16), shp(BF16)],
        compiler_params=_params(1),
        name="peer_score",
    )(h2t, wqt, keys)


def _expert_kernel(h2t_ref, u_ref, vt_ref, n_ref, e1_ref, r2_ref, e2_ref, x1_ref, gt_ref, gpost_ref,
                   o_ref, acc_ref, at_ref, ga_ref, *, eb, ec):
    i = pl.program_id(1)

    @pl.when(i == 0)
    def _():
        acc_ref[...] = jnp.zeros_like(acc_ref)

    tt = h2t_ref.shape[1]
    per = ec // N_KEYS
    n_chunks = eb // ec
    pk = N_KEYS // BF16_ROWS

    def up(c, ls):
        at = jnp.dot(u_ref[c * ec:(c + 1) * ec, :], h2t_ref[:, ls], preferred_element_type=F32)
        at_ref[c % 2, :, ls] = at

    def down(c, ls):
        p = c // DOWN_CHUNKS
        acc_ref[:, ls] += jnp.dot(vt_ref[:, p * DOWN_CHUNKS * ec:(p + 1) * DOWN_CHUNKS * ec],
                                  ga_ref[p % 2, :, ls],
                                  preferred_element_type=F32)

    def gate(c, aa, ls):
        al = c * per + aa
        rows = slice(aa * N_KEYS, (aa + 1) * N_KEYS)
        ga_rows = slice(((c % DOWN_CHUNKS) * per + aa) * N_KEYS, ((c % DOWN_CHUNKS) * per + aa + 1) * N_KEYS)
        g = None
        for h in range(PEER_HEADS):
            nrow = jnp.broadcast_to(n_ref[h, al:al + 1, ls], (BF16_ROWS, GATE_LANES)).astype(BF16)
            erow = jnp.broadcast_to(e1_ref[h, al:al + 1, ls], (BF16_ROWS, GATE_LANES)).astype(BF16)
            term = jnp.where(r2_ref[h, :, :, ls] < nrow[None], e2_ref[h, :, :, ls] * erow[None], 0.0)
            g = term if g is None else g + term
        at = at_ref[c % 2, rows, ls]
        act = (at * (1.0 + lax.erf(at * SQRT_HALF))).astype(BF16)
        ga_ref[(c // DOWN_CHUNKS) % 2, ga_rows, ls] = g.reshape(N_KEYS, GATE_LANES) * act

    slabs = [slice(l0, l0 + GATE_LANES) for l0 in range(0, tt, GATE_LANES)]
    for ls in slabs:
        up(0, ls)
    for c in range(n_chunks):
        for ls in slabs:
            gate(c, 0, ls)
            if c + 1 < n_chunks:
                up(c + 1, ls)
            for aa in range(1, per):
                gate(c, aa, ls)
            if c % DOWN_CHUNKS == DOWN_CHUNKS - 1:
                down(c, ls)

    @pl.when(i == pl.num_programs(1) - 1)
    def _():
        y = acc_ref[...].T
        o_ref[...] = x1_ref[...] + gt_ref[0] * _rms(y, gpost_ref[...])


def _experts(h2t, u, vt, n, e1, r2, e2, x1, gt2, gpost, seq, tt, eb, ec=256):
    d, t = h2t.shape
    ne = u.shape[0]
    a = eb // N_KEYS
    pk = N_KEYS // BF16_ROWS
    sub = pl.BlockSpec((PEER_HEADS, a, tt), lambda j, i: (0, i, j))
    allk = pl.BlockSpec((PEER_HEADS, pk, BF16_ROWS, tt), lambda j, i: (0, 0, 0, j))
    tiled = lambda v: v.reshape(PEER_HEADS, pk, BF16_ROWS, t)
    return pl.pallas_call(
        functools.partial(_expert_kernel, eb=eb, ec=ec),
        grid=(t // tt, ne // eb),
        in_specs=[pl.BlockSpec((d, tt), lambda j, i: (0, j)),
                  pl.BlockSpec((eb, d), lambda j, i: (i, 0)),
                  pl.BlockSpec((d, eb), lambda j, i: (0, i)),
                  sub, sub, allk, allk,
                  pl.BlockSpec((tt, d), lambda j, i: (j, 0)),
                  pl.BlockSpec((1, 1, d), lambda j, i: ((j * tt) // seq, 0, 0)),
                  pl.BlockSpec((1, d), lambda j, i: (0, 0))],
        out_specs=pl.BlockSpec((tt, d), lambda j, i: (j, 0)),
        out_shape=jax.ShapeDtypeStruct((t, d), F32),
        scratch_shapes=[pltpu.VMEM((d, tt), F32),
                        pltpu.VMEM((2, ec, tt), F32),
                        pltpu.VMEM((2, DOWN_CHUNKS * ec, tt), BF16)],
        compiler_params=_params(2),
        name="peer_experts",
    )(h2t, u, vt, n, e1, tiled(r2), tiled(e2), x1, gt2, gpost)


def _rope_table(seq):
    t = jnp.arange(seq, dtype=jnp.int32)
    row = (t // GRID_W).astype(F32)
    col = (t % GRID_W).astype(F32)
    freqs = ROPE_BASE ** (-jnp.arange(0, ROPE_AXIS_DIM, 2, dtype=F32) / ROPE_AXIS_DIM)
    ang_r = row[:, None] * freqs[None, :]
    ang_c = col[:, None] * freqs[None, :]
    zero = jnp.zeros_like(ang_r)
    cos = jnp.concatenate([jnp.cos(ang_r), jnp.cos(ang_r), jnp.cos(ang_c), jnp.cos(ang_c)], axis=1)
    s_lo = jnp.concatenate([-jnp.sin(ang_r), zero, -jnp.sin(ang_c), zero], axis=1)
    s_hi = jnp.concatenate([zero, jnp.sin(ang_r), zero, jnp.sin(ang_c)], axis=1)
    two = lambda a: jnp.concatenate([a, a], axis=1)
    return jnp.concatenate([two(cos), two(s_lo), two(s_hi)], axis=1)


def _layer(x2, ctx2, cond, rope_tab, batch, seq, ctx_len,
           w_ada, b_ada, g_pre_mix, g_post_mix, g_pre_ffn, g_post_ffn, w_in, attn_sink, w_o_attn,
           conv_dw, conv_dw_b, conv_ln_g, conv_ln_b, w_o_conv, w_out, peer_wq, peer_keys, peer_u, peer_v):
    d = D_MODEL
    t = batch * seq
    tm = min(512, seq)
    row = lambda v: v.reshape(1, -1)

    mod = _ada(cond, w_ada, b_ada)
    per_batch = [mod[:batch, j * d:(j + 1) * d].reshape(batch, 1, d) for j in range(6)]
    sh1, sc1, gt1, sh2, sc2, gt2 = per_batch
    csh1 = mod[batch:batch + 1, 0:d]
    csc1 = mod[batch:batch + 1, d:2 * d]

    wq = w_in[:, :K_OFF].reshape(d, N_KV_HEADS, GROUP, HEAD_DIM).transpose(0, 2, 1, 3).reshape(d, ATTN_WIDTH)
    w_in_b = jnp.concatenate([wq, w_in[:, K_OFF:]], axis=1).astype(BF16)
    woa = w_o_attn.reshape(N_KV_HEADS, GROUP, HEAD_DIM, d).transpose(1, 0, 2, 3).reshape(ATTN_WIDTH, d).astype(BF16)

    q, k, v, u, gates = _in_proj(x2, sh1, sc1, row(g_pre_mix), rope_tab, w_in_b, seq, tm)
    kx, vx = _ctx_proj(ctx2, csh1, csc1, row(g_pre_mix), w_in_b[:, K_OFF:CONV_OFF], min(512, ctx2.shape[0]))
    o = _attention(attn_sink, q, k, v, kx, vx, batch, seq, ctx_len)
    hc = _conv(u, conv_dw, row(conv_dw_b), row(conv_ln_g), row(conv_ln_b), seq, min(256, seq))
    x1, h2t = _merge(o, hc, gates, x2, gt1, sh2, sc2, row(g_post_mix), row(g_pre_ffn),
                     woa, w_o_conv.astype(BF16), w_out.astype(BF16), seq, tm)
    n, e1, r2, e2 = _score(h2t, peer_wq.T.astype(BF16), peer_keys.astype(BF16), min(256, seq))
    return _experts(h2t, peer_u.astype(BF16), peer_v.T.astype(BF16), n, e1, r2, e2, x1, gt2,
                    row(g_post_ffn), seq, min(512, seq), 2048)


def kernel(x, c, ctx, c_ctx, w_ada, b_ada, g_pre_mix, g_post_mix, g_pre_ffn, g_post_ffn, w_in, attn_sink,
           w_o_attn, conv_dw, conv_dw_b, conv_ln_g, conv_ln_b, w_o_conv, w_out, peer_wq, peer_keys, peer_u,
           peer_v):
    batch, seq, d = x.shape
    ctx_len = ctx.shape[1]
    depth = w_ada.shape[0]
    assert depth == 1, "context-stream update between layers is not implemented"
    assert d == D_MODEL and seq % ATTN_Q == 0 and seq % GRID_W == 0 and batch < 16
    cond = jnp.zeros((16, d), F32).at[:batch].set(c).at[batch].set(c_ctx)
    rope_tab = _rope_table(seq)
    out = _layer(x.reshape(batch * seq, d), ctx.reshape(batch * ctx_len, d), cond, rope_tab, batch, seq, ctx_len,
                 w_ada[0], b_ada[0], g_pre_mix[0], g_post_mix[0], g_pre_ffn[0], g_post_ffn[0], w_in[0],
                 attn_sink[0], w_o_attn[0], conv_dw[0], conv_dw_b[0], conv_ln_g[0], conv_ln_b[0], w_o_conv[0],
                 w_out[0], peer_wq[0], peer_keys[0], peer_u[0], peer_v[0])
    return out.reshape(batch, seq, d)
```

```python
import functools

import jax
import jax.numpy as jnp
from jax import lax
from jax.experimental import pallas as pl
from jax.experimental.pallas import tpu as pltpu

F32 = jnp.float32
BF16 = jnp.bfloat16

D_MODEL = 1024
GRID_W = 64
N_HEADS = 16
N_KV_HEADS = 4
GROUP = N_HEADS // N_KV_HEADS
HEAD_DIM = 64
ATTN_WIDTH = N_HEADS * HEAD_DIM
KV_WIDTH = N_KV_HEADS * HEAD_DIM
WINDOW = 128
BLOCK = 128
ROPE_AXIS_DIM = HEAD_DIM // 2
ROPE_BASE = 10000.0
CONV_CH = D_MODEL // 2
CONV_WIDTH = 31
CONV_PAD = CONV_WIDTH // 2
K_OFF = ATTN_WIDTH
V_OFF = K_OFF + KV_WIDTH
CONV_OFF = V_OFF + KV_WIDTH
GATE_OFF = CONV_OFF + 2 * CONV_CH
IN_WIDTH = GATE_OFF + 2 * D_MODEL
PEER_HEADS = 8
N_KEYS = 128
N_EXPERTS = N_KEYS * N_KEYS
PEER_QDIM = 256
PEER_HALF = PEER_QDIM // 2
PEER_TOPK = 16
NORM_EPS = 1e-6
NEG_INF = -1e30
NEG_BIG = -3.0e38
SQRT_HALF = 0.7071067811865476
Q_SCALE = HEAD_DIM ** -0.5
LOG2E = 1.4426950408889634

LANES = 128
ATTN_Q = 2 * BLOCK
SUBLANES = 8
BF16_ROWS = 16
HALO = 16
GATE_LANES = 256
DOWN_CHUNKS = 2
VMEM_LIMIT = 56 * 1024 * 1024

_NT = (((1,), (1,)), ((), ()))
_TN = (((0,), (0,)), ((), ()))


def _params(n_axes, vmem=VMEM_LIMIT, flags=None):
    return pltpu.CompilerParams(dimension_semantics=("arbitrary",) * n_axes, vmem_limit_bytes=vmem, flags=flags)


def _rms(x, g):
    return x * lax.rsqrt(jnp.mean(x * x, axis=-1, keepdims=True) + NORM_EPS) * g


def _sigmoid(x):
    return jax.nn.sigmoid(x)


def _ada_kernel(c_ref, w_ref, b_ref, o_ref):
    c = c_ref[...]
    s = (c * _sigmoid(c)).astype(BF16)
    o_ref[...] = jnp.dot(s, w_ref[...].astype(BF16), preferred_element_type=F32) + b_ref[...]


def _ada(cond, w_ada, b_ada):
    rows, d = cond.shape
    n = w_ada.shape[1]
    bn = 1536
    return pl.pallas_call(
        _ada_kernel,
        grid=(n // bn,),
        in_specs=[pl.BlockSpec((rows, d), lambda j: (0, 0)),
                  pl.BlockSpec((d, bn), lambda j: (0, j)),
                  pl.BlockSpec((1, bn), lambda j: (0, j))],
        out_specs=pl.BlockSpec((rows, bn), lambda j: (0, j)),
        out_shape=jax.ShapeDtypeStruct((rows, n), F32),
        compiler_params=_params(1),
        name="ada",
    )(cond, w_ada, b_ada.reshape(1, n))


def _in_kernel(x_ref, sh_ref, sc_ref, g_ref, tab_ref, w_ref, q_ref, k_ref, v_ref, u_ref, gate_ref):
    h = (_rms(x_ref[...], g_ref[...]) * (1.0 + sc_ref[0]) + sh_ref[0]).astype(BF16)
    cos = tab_ref[:, 0:LANES]
    sin_lo = tab_ref[:, LANES:2 * LANES]
    sin_hi = tab_ref[:, 2 * LANES:3 * LANES]

    def proj(lo, n):
        return jnp.dot(h, w_ref[:, lo:lo + n], preferred_element_type=F32)

    def rope(p):
        return (p * cos + pltpu.roll(p, LANES - ROPE_AXIS_DIM // 2, 1) * sin_lo
                + pltpu.roll(p, ROPE_AXIS_DIM // 2, 1) * sin_hi)

    def rope2(p):
        return jnp.concatenate([rope(p[:, :LANES]), rope(p[:, LANES:])], axis=1)

    for g in range(GROUP):
        q_ref[g] = (rope2(proj(g * KV_WIDTH, KV_WIDTH)) * (Q_SCALE * LOG2E)).astype(BF16)
    k_ref[...] = rope2(proj(K_OFF, KV_WIDTH)).astype(BF16)
    v_ref[...] = proj(V_OFF, KV_WIDTH).astype(BF16)
    u_ref[...] = proj(CONV_OFF, CONV_CH) * _sigmoid(proj(CONV_OFF + CONV_CH, CONV_CH))
    for j in range(4):
        gate_ref[:, j * 512:(j + 1) * 512] = _sigmoid(proj(GATE_OFF + j * 512, 512)).astype(BF16)


def _in_proj(x2, sh, sc, g, tab, w, seq, tm):
    t, d = x2.shape
    return pl.pallas_call(
        _in_kernel,
        grid=(t // tm,),
        in_specs=[pl.BlockSpec((tm, d), lambda i: (i, 0)),
                  pl.BlockSpec((1, 1, d), lambda i: ((i * tm) // seq, 0, 0)),
                  pl.BlockSpec((1, 1, d), lambda i: ((i * tm) // seq, 0, 0)),
                  pl.BlockSpec((1, d), lambda i: (0, 0)),
                  pl.BlockSpec((tm, 3 * LANES), lambda i: (i % (seq // tm), 0)),
                  pl.BlockSpec((d, IN_WIDTH), lambda i: (0, 0))],
        out_specs=[pl.BlockSpec((GROUP, tm, KV_WIDTH), lambda i: (0, i, 0)),
                   pl.BlockSpec((tm, KV_WIDTH), lambda i: (i, 0)),
                   pl.BlockSpec((tm, KV_WIDTH), lambda i: (i, 0)),
                   pl.BlockSpec((tm, CONV_CH), lambda i: (i, 0)),
                   pl.BlockSpec((tm, 2 * d), lambda i: (i, 0))],
        out_shape=[jax.ShapeDtypeStruct((GROUP, t, KV_WIDTH), BF16),
                   jax.ShapeDtypeStruct((t, KV_WIDTH), BF16),
                   jax.ShapeDtypeStruct((t, KV_WIDTH), BF16),
                   jax.ShapeDtypeStruct((t, CONV_CH), F32),
                   jax.ShapeDtypeStruct((t, 2 * d), BF16)],
        compiler_params=_params(1),
        name="in_proj",
    )(x2, sh, sc, g, tab, w)


def _ctx_kernel(x_ref, sh_ref, sc_ref, g_ref, w_ref, k_ref, v_ref):
    h = (_rms(x_ref[...], g_ref[...]) * (1.0 + sc_ref[...]) + sh_ref[...]).astype(BF16)
    k_ref[...] = jnp.dot(h, w_ref[:, :KV_WIDTH], preferred_element_type=F32).astype(BF16)
    v_ref[...] = jnp.dot(h, w_ref[:, KV_WIDTH:], preferred_element_type=F32).astype(BF16)


def _ctx_proj(c2, sh, sc, g, w, tm):
    t, d = c2.shape
    return pl.pallas_call(
        _ctx_kernel,
        grid=(t // tm,),
        in_specs=[pl.BlockSpec((tm, d), lambda i: (i, 0)),
                  pl.BlockSpec((1, d), lambda i: (0, 0)),
                  pl.BlockSpec((1, d), lambda i: (0, 0)),
                  pl.BlockSpec((1, d), lambda i: (0, 0)),
                  pl.BlockSpec((d, 2 * KV_WIDTH), lambda i: (0, 0))],
        out_specs=[pl.BlockSpec((tm, KV_WIDTH), lambda i: (i, 0)),
                   pl.BlockSpec((tm, KV_WIDTH), lambda i: (i, 0))],
        out_shape=[jax.ShapeDtypeStruct((t, KV_WIDTH), BF16),
                   jax.ShapeDtypeStruct((t, KV_WIDTH), BF16)],
        compiler_params=_params(1),
        name="ctx_proj",
    )(c2, sh, sc, g, w)


def _attn_kernel(sink_ref, q_ref, kp_ref, kc_ref, kn_ref, vp_ref, vc_ref, vn_ref, kx_ref, vx_ref,
                 o_ref, *, n_steps):
    n = pl.program_id(1)
    qs = q_ref[...].reshape(GROUP * ATTN_Q, KV_WIDTH)
    kw = jnp.concatenate([kp_ref[...], kc_ref[...], kn_ref[...], kx_ref[...]], axis=0)
    vw = jnp.concatenate([vp_ref[...], vc_ref[...], vn_ref[...], vx_ref[...]], axis=0)
    nk = kw.shape[0]
    nwin = ATTN_Q + 2 * BLOCK
    kj = lax.broadcasted_iota(jnp.int32, (nwin, ATTN_Q), 0)
    qi = lax.broadcasted_iota(jnp.int32, (nwin, ATTN_Q), 1)
    ok = (kj >= qi) & (kj <= qi + 2 * WINDOW)
    ok = ok & ((kj >= BLOCK) | (n > 0)) & ((kj < nwin - BLOCK) | (n < n_steps - 1))
    bias = jnp.concatenate([jnp.where(ok, 0.0, NEG_INF).astype(F32)] * GROUP, axis=1)
    lane = lax.broadcasted_iota(jnp.int32, (nk, KV_WIDTH), 1)
    scores = []
    for h in range(N_KV_HEADS):
        km = jnp.where(lane // HEAD_DIM == h, kw, jnp.zeros_like(kw))
        st = lax.dot_general(km, qs, _NT, preferred_element_type=F32)
        scores.append(jnp.concatenate([st[:nwin] + bias, st[nwin:]], axis=0))
    vt = vw.astype(F32).T.astype(BF16)
    ones = jnp.ones((BF16_ROWS, nk), BF16)
    heads = []
    for h in range(N_KV_HEADS):
        st = scores[h]
        sink = jnp.concatenate(
            [jnp.full((1, ATTN_Q), sink_ref[h * GROUP + g] * LOG2E, F32) for g in range(GROUP)], axis=1)
        m = jnp.maximum(jnp.max(st, axis=0, keepdims=True), sink)
        pt = jnp.exp2(st - m).astype(BF16)
        lhs = jnp.concatenate([vt[h * HEAD_DIM:(h + 1) * HEAD_DIM], ones], axis=0)
        rest = jnp.dot(lhs, pt, preferred_element_type=F32)
        denom = rest[HEAD_DIM:HEAD_DIM + 1] + jnp.exp2(sink - m)
        heads.append(rest[:HEAD_DIM] * (1.0 / denom))
    out = jnp.concatenate(heads, axis=0).T
    o_ref[...] = out.reshape(GROUP, ATTN_Q, KV_WIDTH).astype(BF16)


def _attention(sink, q, k, v, kx, vx, batch, seq, ctx_len):
    nb = seq // BLOCK
    per = ATTN_Q // BLOCK
    steps = seq // ATTN_Q
    t = batch * seq

    def cur(b, n):
        return (b * steps + n, 0)

    def prev(b, n):
        return (b * nb + jnp.maximum(n * per - 1, 0), 0)

    def nxt(b, n):
        return (b * nb + jnp.minimum((n + 1) * per, nb - 1), 0)

    kv = lambda f: pl.BlockSpec((BLOCK, KV_WIDTH), f)
    kc = pl.BlockSpec((ATTN_Q, KV_WIDTH), cur)
    cx = pl.BlockSpec((ctx_len, KV_WIDTH), lambda b, n: (b, 0))
    qo = pl.BlockSpec((GROUP, ATTN_Q, KV_WIDTH), lambda b, n: (0, b * steps + n, 0))
    return pl.pallas_call(
        functools.partial(_attn_kernel, n_steps=steps),
        grid=(batch, steps),
        in_specs=[pl.BlockSpec(memory_space=pltpu.SMEM), qo,
                  kv(prev), kc, kv(nxt), kv(prev), kc, kv(nxt), cx, cx],
        out_specs=qo,
        out_shape=jax.ShapeDtypeStruct((GROUP, t, KV_WIDTH), BF16),
        compiler_params=_params(2),
        name="attn",
    )(sink, q, k, k, k, v, v, v, kx, vx)


def _conv_kernel(up_ref, uc_ref, un_ref, dw_ref, dwb_ref, lng_ref, lnb_ref, o_ref, xp_ref, ph_ref,
                 *, tiles_per_seq, tc, rows):
    i = pl.program_id(0)
    first = (i % tiles_per_seq) == 0
    last = (i % tiles_per_seq) == tiles_per_seq - 1
    xp_ref[0:HALO, :] = jnp.where(first, 0.0, up_ref[...])
    xp_ref[HALO:HALO + tc, :] = uc_ref[...]
    xp_ref[HALO + tc:2 * HALO + tc, :] = jnp.where(last, 0.0, un_ref[...])
    span = ph_ref.shape[1]
    for r in range(SUBLANES):
        ph_ref[r] = xp_ref[r:r + span, :]
    for c in range(tc // rows):
        acc = jnp.zeros((rows, CONV_CH), F32)
        for k in range(CONV_WIDTH):
            q, r = divmod(k + HALO - CONV_PAD, SUBLANES)
            r0 = c * rows + q * SUBLANES
            acc = acc + ph_ref[r, r0:r0 + rows, :] * dw_ref[k:k + 1, :]
        acc = acc + dwb_ref[...]
        mu = jnp.mean(acc, axis=-1, keepdims=True)
        xc = acc - mu
        var = jnp.mean(xc * xc, axis=-1, keepdims=True)
        y = xc * lax.rsqrt(var + NORM_EPS) * lng_ref[...] + lnb_ref[...]
        o_ref[c * rows:(c + 1) * rows, :] = (y * _sigmoid(y)).astype(BF16)


def _conv(u, dw, dwb, lng, lnb, seq, tc, rows=64):
    t, ch = u.shape
    per = tc // HALO
    nh = t // HALO
    return pl.pallas_call(
        functools.partial(_conv_kernel, tiles_per_seq=seq // tc, tc=tc, rows=rows),
        grid=(t // tc,),
        in_specs=[pl.BlockSpec((HALO, ch), lambda i: (jnp.maximum(i * per - 1, 0), 0)),
                  pl.BlockSpec((tc, ch), lambda i: (i, 0)),
                  pl.BlockSpec((HALO, ch), lambda i: (jnp.minimum((i + 1) * per, nh - 1), 0)),
                  pl.BlockSpec((CONV_WIDTH, ch), lambda i: (0, 0)),
                  pl.BlockSpec((1, ch), lambda i: (0, 0)),
                  pl.BlockSpec((1, ch), lambda i: (0, 0)),
                  pl.BlockSpec((1, ch), lambda i: (0, 0))],
        out_specs=pl.BlockSpec((tc, ch), lambda i: (i, 0)),
        out_shape=jax.ShapeDtypeStruct((t, ch), BF16),
        scratch_shapes=[pltpu.VMEM((tc + 2 * HALO, ch), F32),
                        pltpu.VMEM((SUBLANES, tc + 2 * HALO - SUBLANES, ch), F32)],
        compiler_params=_params(1),
        name="conv",
    )(u, u, u, dw, dwb, lng, lnb)


def _merge_kernel(o_ref, hc_ref, gate_ref, x_ref, gt_ref, sh_ref, sc_ref, gpost_ref, gpre_ref,
                  woa_ref, woc_ref, wout_ref, x1_ref, h2t_ref):
    d = x_ref.shape[1]
    y_attn = jnp.dot(o_ref[0], woa_ref[0:KV_WIDTH, :], preferred_element_type=F32)
    for g in range(1, GROUP):
        y_attn = y_attn + jnp.dot(o_ref[g], woa_ref[g * KV_WIDTH:(g + 1) * KV_WIDTH, :],
                                  preferred_element_type=F32)
    y_conv = jnp.dot(hc_ref[...], woc_ref[...], preferred_element_type=F32)
    mix = (gate_ref[:, :d].astype(F32) * y_attn + gate_ref[:, d:].astype(F32) * y_conv).astype(BF16)
    mixed = jnp.dot(mix, wout_ref[...], preferred_element_type=F32)
    x1 = x_ref[...] + gt_ref[0] * _rms(mixed, gpost_ref[...])
    x1_ref[...] = x1
    h2 = _rms(x1, gpre_ref[...]) * (1.0 + sc_ref[0]) + sh_ref[0]
    h2t_ref[...] = h2.T.astype(BF16)


def _merge(o, hc, gates, x2, gt1, sh2, sc2, gpost, gpre, woa, woc, wout, seq, tm):
    t, d = x2.shape
    mod = pl.BlockSpec((1, 1, d), lambda i: ((i * tm) // seq, 0, 0))
    vec = pl.BlockSpec((1, d), lambda i: (0, 0))
    full = lambda a: pl.BlockSpec(a.shape, lambda i: (0, 0))
    return pl.pallas_call(
        _merge_kernel,
        grid=(t // tm,),
        in_specs=[pl.BlockSpec((GROUP, tm, KV_WIDTH), lambda i: (0, i, 0)),
                  pl.BlockSpec((tm, CONV_CH), lambda i: (i, 0)),
                  pl.BlockSpec((tm, 2 * d), lambda i: (i, 0)),
                  pl.BlockSpec((tm, d), lambda i: (i, 0)),
                  mod, mod, mod, vec, vec, full(woa), full(woc), full(wout)],
        out_specs=[pl.BlockSpec((tm, d), lambda i: (i, 0)),
                   pl.BlockSpec((d, tm), lambda i: (0, i))],
        out_shape=[jax.ShapeDtypeStruct((t, d), F32),
                   jax.ShapeDtypeStruct((d, t), BF16)],
        compiler_params=_params(1),
        name="merge",
    )(o, hc, gates, x2, gt1, sh2, sc2, gpost, gpre, woa, woc, wout)


def _top16(s):
    rem = s
    rank = jnp.full(s.shape, float(PEER_TOPK), F32)
    vals = []
    for j in range(PEER_TOPK):
        m = jnp.max(rem, axis=0, keepdims=True)
        eq = rem == m
        rank = jnp.where(eq, float(j), rank)
        rem = jnp.where(eq, NEG_BIG, rem)
        vals.append(m)
    return jnp.concatenate(vals, axis=0), rank


_SORT16 = ((0, 1), (2, 3), (0, 2), (1, 3), (1, 2), (4, 5), (6, 7), (4, 6), (5, 7), (5, 6), (0, 4), (2, 6), (2, 4),
           (1, 5), (3, 7), (3, 5), (1, 2), (3, 4), (5, 6), (8, 9), (10, 11), (8, 10), (9, 11), (9, 10), (12, 13),
           (14, 15), (12, 14), (13, 15), (13, 14), (8, 12), (10, 14), (10, 12), (9, 13), (11, 15), (11, 13), (9, 10),
           (11, 12), (13, 14), (0, 8), (4, 12), (4, 8), (2, 10), (6, 14), (6, 10), (2, 4), (6, 8), (10, 12), (1, 9),
           (5, 13), (5, 9), (3, 11), (7, 15), (7, 11), (3, 5), (7, 9), (11, 13), (1, 2), (3, 4), (5, 6), (7, 8),
           (9, 10), (11, 12), (13, 14))


def _top16_values(s):
    cols = [s[k * SUBLANES:(k + 1) * SUBLANES] for k in range(N_KEYS // SUBLANES)]
    for i, j in _SORT16:
        cols[i], cols[j] = jnp.maximum(cols[i], cols[j]), jnp.minimum(cols[i], cols[j])
    vals = []
    for j in range(PEER_TOPK):
        m = jnp.max(cols[0], axis=0, keepdims=True)
        vals.append(m)
        hit = cols[0] == m
        for k in range(PEER_TOPK - 1 - j):
            cols[k] = jnp.where(hit, cols[k + 1], cols[k])
    return jnp.concatenate(vals, axis=0)


def _kth_largest(c, k):
    rem = c
    m = None
    for j in range(k):
        m = jnp.max(rem, axis=0, keepdims=True)
        if j + 1 < k:
            rem = jnp.where(rem == m, NEG_BIG, rem)
    return m


def _retrieve(s1, s2):
    v1 = _top16_values(s1)
    v2, rank2 = _top16(s2)
    half = PEER_TOPK // 2
    cand = jnp.concatenate([v1[0:1] + v2]
                           + [v1[i:i + 1] + v2[0:half] for i in range(1, half)]
                           + [v1[half:] + v2[0:1]], axis=0)
    thr = _kth_largest(cand, PEER_TOPK)
    ev1 = jnp.exp(v1 - v1[0:1])
    ev2 = jnp.exp(v2 - v2[0:1])
    z = jnp.zeros_like(thr)
    n = jnp.zeros(s1.shape, F32)
    for i in range(PEER_TOPK):
        sel = (v1[i:i + 1] + v2) >= thr
        cnt = jnp.sum(jnp.where(sel, 1.0, 0.0), axis=0, keepdims=True)
        z = z + ev1[i:i + 1] * jnp.sum(jnp.where(sel, ev2, 0.0), axis=0, keepdims=True)
        n = jnp.where(s1 == v1[i:i + 1], cnt, n)
    e1 = jnp.where(s1 >= v1[PEER_TOPK - 1:], jnp.exp(s1 - v1[0:1]) * (0.5 / z), 0.0)
    return n, e1, rank2.astype(BF16), jnp.exp(s2 - v2[0:1]).astype(BF16)


def _score_kernel(h2t_ref, wqt_ref, keys_ref, n_ref, e1_ref, r2_ref, e2_ref):
    qpt = jnp.dot(wqt_ref[...], h2t_ref[...], preferred_element_type=F32)
    for h in range(PEER_HEADS):
        q1 = qpt[h * PEER_QDIM:h * PEER_QDIM + PEER_HALF].astype(BF16)
        q2 = qpt[h * PEER_QDIM + PEER_HALF:(h + 1) * PEER_QDIM].astype(BF16)
        s1 = jnp.dot(keys_ref[h, 0], q1, preferred_element_type=F32)
        s2 = jnp.dot(keys_ref[h, 1], q2, preferred_element_type=F32)
        for l0 in range(0, s1.shape[1], LANES):
            ls = slice(l0, l0 + LANES)
            n_ref[h, :, ls], e1_ref[h, :, ls], r2_ref[h, :, ls], e2_ref[h, :, ls] = _retrieve(s1[:, ls], s2[:, ls])


def _score(h2t, wqt, keys, tt):
    d, t = h2t.shape
    blk = pl.BlockSpec((PEER_HEADS, N_KEYS, tt), lambda j: (0, 0, j))
    shp = lambda dt: jax.ShapeDtypeStruct((PEER_HEADS, N_KEYS, t), dt)
    return pl.pallas_call(
        _score_kernel,
        grid=(t // tt,),
        in_specs=[pl.BlockSpec((d, tt), lambda j: (0, j)),
                  pl.BlockSpec(wqt.shape, lambda j: (0, 0)),
                  pl.BlockSpec(keys.shape, lambda j: (0, 0, 0, 0))],
        out_specs=[blk, blk, blk, blk],
        out_shape=[shp(F32), shp(F32), shp(BF16), shp(BF16)],
        compiler_params=_params(1),
        name="peer_score",
    )(h2t, wqt, keys)


def _expert_kernel(h2t_ref, u_ref, vt_ref, n_ref, e1_ref, r2_ref, e2_ref, x1_ref, gt_ref, gpost_ref,
                   o_ref, acc_ref, at_ref, ga_ref, *, eb, ec):
    i = pl.program_id(1)

    @pl.when(i == 0)
    def _():
        acc_ref[...] = jnp.zeros_like(acc_ref)

    tt = h2t_ref.shape[1]
    per = ec // N_KEYS
    n_chunks = eb // ec
    pk = N_KEYS // BF16_ROWS

    def up(c, ls):
        at = jnp.dot(u_ref[c * ec:(c + 1) * ec, :], h2t_ref[:, ls], preferred_element_type=F32)
        at_ref[c % 2, :, ls] = at

    def down(c, ls):
        p = c // DOWN_CHUNKS
        acc_ref[:, ls] += jnp.dot(vt_ref[:, p * DOWN_CHUNKS * ec:(p + 1) * DOWN_CHUNKS * ec],
                                  ga_ref[p % 2, :, ls],
                                  preferred_element_type=F32)

    def gate(c, aa, ls):
        al = c * per + aa
        rows = slice(aa * N_KEYS, (aa + 1) * N_KEYS)
        ga_rows = slice(((c % DOWN_CHUNKS) * per + aa) * N_KEYS, ((c % DOWN_CHUNKS) * per + aa + 1) * N_KEYS)
        g = None
        for h in range(PEER_HEADS):
            nrow = jnp.broadcast_to(n_ref[h, al:al + 1, ls], (BF16_ROWS, GATE_LANES)).astype(BF16)
            erow = jnp.broadcast_to(e1_ref[h, al:al + 1, ls], (BF16_ROWS, GATE_LANES)).astype(BF16)
            term = jnp.where(r2_ref[h, :, :, ls] < nrow[None], e2_ref[h, :, :, ls] * erow[None], 0.0)
            g = term if g is None else g + term
        at = at_ref[c % 2, rows, ls]
        act = (at * (1.0 + lax.erf(at * SQRT_HALF))).astype(BF16)
        ga_ref[(c // DOWN_CHUNKS) % 2, ga_rows, ls] = g.reshape(N_KEYS, GATE_LANES) * act

    slabs = [slice(l0, l0 + GATE_LANES) for l0 in range(0, tt, GATE_LANES)]
    for ls in slabs:
        up(0, ls)
    for c in range(n_chunks):
        for ls in slabs:
            gate(c, 0, ls)
            if c + 1 < n_chunks:
                up(c + 1, ls)
            for aa in range(1, per):
                gate(c, aa, ls)
            if c % DOWN_CHUNKS == DOWN_CHUNKS - 1:
                down(c, ls)

    @pl.when(i == pl.num_programs(1) - 1)
    def _():
        y = acc_ref[...].T
        o_ref[...] = x1_ref[...] + gt_ref[0] * _rms(y, gpost_ref[...])


def _experts(h2t, u, vt, n, e1, r2, e2, x1, gt2, gpost, seq, tt, eb, ec=256):
    d, t = h2t.shape
    ne = u.shape[0]
    a = eb // N_KEYS
    pk = N_KEYS // BF16_ROWS
    sub = pl.BlockSpec((PEER_HEADS, a, tt), lambda j, i: (0, i, j))
    allk = pl.BlockSpec((PEER_HEADS, pk, BF16_ROWS, tt), lambda j, i: (0, 0, 0, j))
    tiled = lambda v: v.reshape(PEER_HEADS, pk, BF16_ROWS, t)
    return pl.pallas_call(
        functools.partial(_expert_kernel, eb=eb, ec=ec),
        grid=(t // tt, ne // eb),
        in_specs=[pl.BlockSpec((d, tt), lambda j, i: (0, j)),
                  pl.BlockSpec((eb, d), lambda j, i: (i, 0)),
                  pl.BlockSpec((d, eb), lambda j, i: (0, i)),
                  sub, sub, allk, allk,
                  pl.BlockSpec((tt, d), lambda j, i: (j, 0)),
                  pl.BlockSpec((1, 1, d), lambda j, i: ((j * tt) // seq, 0, 0)),
                  pl.BlockSpec((1, d), lambda j, i: (0, 0))],
        out_specs=pl.BlockSpec((tt, d), lambda j, i: (j, 0)),
        out_shape=jax.ShapeDtypeStruct((t, d), F32),
        scratch_shapes=[pltpu.VMEM((d, tt), F32),
                        pltpu.VMEM((2, ec, tt), F32),
                        pltpu.VMEM((2, DOWN_CHUNKS * ec, tt), BF16)],
        compiler_params=_params(2),
        name="peer_experts",
    )(h2t, u, vt, n, e1, tiled(r2), tiled(e2), x1, gt2, gpost)


def _rope_table(seq):
    t = jnp.arange(seq, dtype=jnp.int32)
    row = (t // GRID_W).astype(F32)
    col = (t % GRID_W).astype(F32)
    freqs = ROPE_BASE ** (-jnp.arange(0, ROPE_AXIS_DIM, 2, dtype=F32) / ROPE_AXIS_DIM)
    ang_r = row[:, None] * freqs[None, :]
    ang_c = col[:, None] * freqs[None, :]
    zero = jnp.zeros_like(ang_r)
    cos = jnp.concatenate([jnp.cos(ang_r), jnp.cos(ang_r), jnp.cos(ang_c), jnp.cos(ang_c)], axis=1)
    s_lo = jnp.concatenate([-jnp.sin(ang_r), zero, -jnp.sin(ang_c), zero], axis=1)
    s_hi = jnp.concatenate([zero, jnp.sin(ang_r), zero, jnp.sin(ang_c)], axis=1)
    two = lambda a: jnp.concatenate([a, a], axis=1)
    return jnp.concatenate([two(cos), two(s_lo), two(s_hi)], axis=1)


def _layer(x2, ctx2, cond, rope_tab, batch, seq, ctx_len,
           w_ada, b_ada, g_pre_mix, g_post_mix, g_pre_ffn, g_post_ffn, w_in, attn_sink, w_o_attn,
           conv_dw, conv_dw_b, conv_ln_g, conv_ln_b, w_o_conv, w_out, peer_wq, peer_keys, peer_u, peer_v):
    d = D_MODEL
    t = batch * seq
    tm = min(512, seq)
    row = lambda v: v.reshape(1, -1)

    mod = _ada(cond, w_ada, b_ada)
    per_batch = [mod[:batch, j * d:(j + 1) * d].reshape(batch, 1, d) for j in range(6)]
    sh1, sc1, gt1, sh2, sc2, gt2 = per_batch
    csh1 = mod[batch:batch + 1, 0:d]
    csc1 = mod[batch:batch + 1, d:2 * d]

    wq = w_in[:, :K_OFF].reshape(d, N_KV_HEADS, GROUP, HEAD_DIM).transpose(0, 2, 1, 3).reshape(d, ATTN_WIDTH)
    w_in_b = jnp.concatenate([wq, w_in[:, K_OFF:]], axis=1).astype(BF16)
    woa = w_o_attn.reshape(N_KV_HEADS, GROUP, HEAD_DIM, d).transpose(1, 0, 2, 3).reshape(ATTN_WIDTH, d).astype(BF16)

    q, k, v, u, gates = _in_proj(x2, sh1, sc1, row(g_pre_mix), rope_tab, w_in_b, seq, tm)
    kx, vx = _ctx_proj(ctx2, csh1, csc1, row(g_pre_mix), w_in_b[:, K_OFF:CONV_OFF], min(512, ctx2.shape[0]))
    o = _attention(attn_sink, q, k, v, kx, vx, batch, seq, ctx_len)
    hc = _conv(u, conv_dw, row(conv_dw_b), row(conv_ln_g), row(conv_ln_b), seq, min(256, seq))
    x1, h2t = _merge(o, hc, gates, x2, gt1, sh2, sc2, row(g_post_mix), row(g_pre_ffn),
                     woa, w_o_conv.astype(BF16), w_out.astype(BF16), seq, tm)
    n, e1, r2, e2 = _score(h2t, peer_wq.T.astype(BF16), peer_keys.astype(BF16), min(256, seq))
    return _experts(h2t, peer_u.astype(BF16), peer_v.T.astype(BF16), n, e1, r2, e2, x1, gt2,
                    row(g_post_ffn), seq, min(512, seq), 4096)


def kernel(x, c, ctx, c_ctx, w_ada, b_ada, g_pre_mix, g_post_mix, g_pre_ffn, g_post_ffn, w_in, attn_sink,
           w_o_attn, conv_dw, conv_dw_b, conv_ln_g, conv_ln_b, w_o_conv, w_out, peer_wq, peer_keys, peer_u,
           peer_v):
    batch, seq, d = x.shape
    ctx_len = ctx.shape[1]
    depth = w_ada.shape[0]
    assert depth == 1, "context-stream update between layers is not implemented"
    assert d == D_MODEL and seq % ATTN_Q == 0 and seq % GRID_W == 0 and batch < 16
    cond = jnp.zeros((16, d), F32).at[:batch].set(c).at[batch].set(c_ctx)
    rope_tab = _rope_table(seq)
    out = _layer(x.reshape(batch * seq, d), ctx.reshape(batch * ctx_len, d), cond, rope_tab, batch, seq, ctx_len,
                 w_ada[0], b_ada[0], g_pre_mix[0], g_post_mix[0], g_pre_ffn[0], g_post_ffn[0], w_in[0],
                 attn_sink[0], w_o_attn[0], conv_dw[0], conv_dw_b[0], conv_ln_g[0], conv_ln_b[0], w_o_conv[0],
                 w_out[0], peer_wq[0], peer_keys[0], peer_u[0], peer_v[0])
    return out.reshape(batch, seq, d)
```

```python
import functools

import jax
import jax.numpy as jnp
from jax import lax
from jax.experimental import pallas as pl
from jax.experimental.pallas import tpu as pltpu

F32 = jnp.float32
BF16 = jnp.bfloat16

D_MODEL = 1024
GRID_W = 64
N_HEADS = 16
N_KV_HEADS = 4
GROUP = N_HEADS // N_KV_HEADS
HEAD_DIM = 64
ATTN_WIDTH = N_HEADS * HEAD_DIM
KV_WIDTH = N_KV_HEADS * HEAD_DIM
WINDOW = 128
BLOCK = 128
ROPE_AXIS_DIM = HEAD_DIM // 2
ROPE_BASE = 10000.0
CONV_CH = D_MODEL // 2
CONV_WIDTH = 31
CONV_PAD = CONV_WIDTH // 2
K_OFF = ATTN_WIDTH
V_OFF = K_OFF + KV_WIDTH
CONV_OFF = V_OFF + KV_WIDTH
GATE_OFF = CONV_OFF + 2 * CONV_CH
IN_WIDTH = GATE_OFF + 2 * D_MODEL
PEER_HEADS = 8
N_KEYS = 128
N_EXPERTS = N_KEYS * N_KEYS
PEER_QDIM = 256
PEER_HALF = PEER_QDIM // 2
PEER_TOPK = 16
NORM_EPS = 1e-6
NEG_INF = -1e30
NEG_BIG = -3.0e38
SQRT_HALF = 0.7071067811865476
Q_SCALE = HEAD_DIM ** -0.5
LOG2E = 1.4426950408889634

LANES = 128
ATTN_Q = 2 * BLOCK
SUBLANES = 8
BF16_ROWS = 16
HALO = 16
GATE_LANES = 256
DOWN_CHUNKS = 2
VMEM_LIMIT = 56 * 1024 * 1024

_NT = (((1,), (1,)), ((), ()))
_TN = (((0,), (0,)), ((), ()))


def _params(n_axes, vmem=VMEM_LIMIT, flags=None):
    return pltpu.CompilerParams(dimension_semantics=("arbitrary",) * n_axes, vmem_limit_bytes=vmem, flags=flags)


def _rms(x, g):
    return x * lax.rsqrt(jnp.mean(x * x, axis=-1, keepdims=True) + NORM_EPS) * g


def _sigmoid(x):
    return jax.nn.sigmoid(x)


def _ada_kernel(c_ref, w_ref, b_ref, o_ref):
    c = c_ref[...]
    s = (c * _sigmoid(c)).astype(BF16)
    o_ref[...] = jnp.dot(s, w_ref[...].astype(BF16), preferred_element_type=F32) + b_ref[...]


def _ada(cond, w_ada, b_ada):
    rows, d = cond.shape
    n = w_ada.shape[1]
    bn = 1536
    return pl.pallas_call(
        _ada_kernel,
        grid=(n // bn,),
        in_specs=[pl.BlockSpec((rows, d), lambda j: (0, 0)),
                  pl.BlockSpec((d, bn), lambda j: (0, j)),
                  pl.BlockSpec((1, bn), lambda j: (0, j))],
        out_specs=pl.BlockSpec((rows, bn), lambda j: (0, j)),
        out_shape=jax.ShapeDtypeStruct((rows, n), F32),
        compiler_params=_params(1),
        name="ada",
    )(cond, w_ada, b_ada.reshape(1, n))


def _in_kernel(x_ref, sh_ref, sc_ref, g_ref, tab_ref, w_ref, q_ref, k_ref, v_ref, u_ref, gate_ref):
    h = (_rms(x_ref[...], g_ref[...]) * (1.0 + sc_ref[0]) + sh_ref[0]).astype(BF16)
    cos = tab_ref[:, 0:LANES]
    sin_lo = tab_ref[:, LANES:2 * LANES]
    sin_hi = tab_ref[:, 2 * LANES:3 * LANES]

    def proj(lo, n):
        return jnp.dot(h, w_ref[:, lo:lo + n], preferred_element_type=F32)

    def rope(p):
        return (p * cos + pltpu.roll(p, LANES - ROPE_AXIS_DIM // 2, 1) * sin_lo
                + pltpu.roll(p, ROPE_AXIS_DIM // 2, 1) * sin_hi)

    def rope2(p):
        return jnp.concatenate([rope(p[:, :LANES]), rope(p[:, LANES:])], axis=1)

    for g in range(GROUP):
        q_ref[g] = (rope2(proj(g * KV_WIDTH, KV_WIDTH)) * (Q_SCALE * LOG2E)).astype(BF16)
    k_ref[...] = rope2(proj(K_OFF, KV_WIDTH)).astype(BF16)
    v_ref[...] = proj(V_OFF, KV_WIDTH).astype(BF16)
    u_ref[...] = proj(CONV_OFF, CONV_CH) * _sigmoid(proj(CONV_OFF + CONV_CH, CONV_CH))
    for j in range(4):
        gate_ref[:, j * 512:(j + 1) * 512] = _sigmoid(proj(GATE_OFF + j * 512, 512)).astype(BF16)


def _in_proj(x2, sh, sc, g, tab, w, seq, tm):
    t, d = x2.shape
    return pl.pallas_call(
        _in_kernel,
        grid=(t // tm,),
        in_specs=[pl.BlockSpec((tm, d), lambda i: (i, 0)),
                  pl.BlockSpec((1, 1, d), lambda i: ((i * tm) // seq, 0, 0)),
                  pl.BlockSpec((1, 1, d), lambda i: ((i * tm) // seq, 0, 0)),
                  pl.BlockSpec((1, d), lambda i: (0, 0)),
                  pl.BlockSpec((tm, 3 * LANES), lambda i: (i % (seq // tm), 0)),
                  pl.BlockSpec((d, IN_WIDTH), lambda i: (0, 0))],
        out_specs=[pl.BlockSpec((GROUP, tm, KV_WIDTH), lambda i: (0, i, 0)),
                   pl.BlockSpec((tm, KV_WIDTH), lambda i: (i, 0)),
                   pl.BlockSpec((tm, KV_WIDTH), lambda i: (i, 0)),
                   pl.BlockSpec((tm, CONV_CH), lambda i: (i, 0)),
                   pl.BlockSpec((tm, 2 * d), lambda i: (i, 0))],
        out_shape=[jax.ShapeDtypeStruct((GROUP, t, KV_WIDTH), BF16),
                   jax.ShapeDtypeStruct((t, KV_WIDTH), BF16),
                   jax.ShapeDtypeStruct((t, KV_WIDTH), BF16),
                   jax.ShapeDtypeStruct((t, CONV_CH), F32),
                   jax.ShapeDtypeStruct((t, 2 * d), BF16)],
        compiler_params=_params(1),
        name="in_proj",
    )(x2, sh, sc, g, tab, w)


def _ctx_kernel(x_ref, sh_ref, sc_ref, g_ref, w_ref, k_ref, v_ref):
    h = (_rms(x_ref[...], g_ref[...]) * (1.0 + sc_ref[...]) + sh_ref[...]).astype(BF16)
    k_ref[...] = jnp.dot(h, w_ref[:, :KV_WIDTH], preferred_element_type=F32).astype(BF16)
    v_ref[...] = jnp.dot(h, w_ref[:, KV_WIDTH:], preferred_element_type=F32).astype(BF16)


def _ctx_proj(c2, sh, sc, g, w, tm):
    t, d = c2.shape
    return pl.pallas_call(
        _ctx_kernel,
        grid=(t // tm,),
        in_specs=[pl.BlockSpec((tm, d), lambda i: (i, 0)),
                  pl.BlockSpec((1, d), lambda i: (0, 0)),
                  pl.BlockSpec((1, d), lambda i: (0, 0)),
                  pl.BlockSpec((1, d), lambda i: (0, 0)),
                  pl.BlockSpec((d, 2 * KV_WIDTH), lambda i: (0, 0))],
        out_specs=[pl.BlockSpec((tm, KV_WIDTH), lambda i: (i, 0)),
                   pl.BlockSpec((tm, KV_WIDTH), lambda i: (i, 0))],
        out_shape=[jax.ShapeDtypeStruct((t, KV_WIDTH), BF16),
                   jax.ShapeDtypeStruct((t, KV_WIDTH), BF16)],
        compiler_params=_params(1),
        name="ctx_proj",
    )(c2, sh, sc, g, w)


def _attn_kernel(sink_ref, q_ref, kp_ref, kc_ref, kn_ref, vp_ref, vc_ref, vn_ref, kx_ref, vx_ref,
                 o_ref, *, n_steps):
    n = pl.program_id(1)
    qs = q_ref[...].reshape(GROUP * ATTN_Q, KV_WIDTH)
    kw = jnp.concatenate([kp_ref[...], kc_ref[...], kn_ref[...], kx_ref[...]], axis=0)
    vw = jnp.concatenate([vp_ref[...], vc_ref[...], vn_ref[...], vx_ref[...]], axis=0)
    nk = kw.shape[0]
    nwin = ATTN_Q + 2 * BLOCK
    kj = lax.broadcasted_iota(jnp.int32, (nwin, ATTN_Q), 0)
    qi = lax.broadcasted_iota(jnp.int32, (nwin, ATTN_Q), 1)
    ok = (kj >= qi) & (kj <= qi + 2 * WINDOW)
    ok = ok & ((kj >= BLOCK) | (n > 0)) & ((kj < nwin - BLOCK) | (n < n_steps - 1))
    bias = jnp.concatenate([jnp.where(ok, 0.0, NEG_INF).astype(F32)] * GROUP, axis=1)
    lane = lax.broadcasted_iota(jnp.int32, (nk, KV_WIDTH), 1)
    scores = []
    for h in range(N_KV_HEADS):
        km = jnp.where(lane // HEAD_DIM == h, kw, jnp.zeros_like(kw))
        st = lax.dot_general(km, qs, _NT, preferred_element_type=F32)
        scores.append(jnp.concatenate([st[:nwin] + bias, st[nwin:]], axis=0))
    vt = vw.astype(F32).T.astype(BF16)
    ones = jnp.ones((BF16_ROWS, nk), BF16)
    heads = []
    for h in range(N_KV_HEADS):
        st = scores[h]
        sink = jnp.concatenate(
            [jnp.full((1, ATTN_Q), sink_ref[h * GROUP + g] * LOG2E, F32) for g in range(GROUP)], axis=1)
        m = jnp.maximum(jnp.max(st, axis=0, keepdims=True), sink)
        pt = jnp.exp2(st - m).astype(BF16)
        lhs = jnp.concatenate([vt[h * HEAD_DIM:(h + 1) * HEAD_DIM], ones], axis=0)
        rest = jnp.dot(lhs, pt, preferred_element_type=F32)
        denom = rest[HEAD_DIM:HEAD_DIM + 1] + jnp.exp2(sink - m)
        heads.append(rest[:HEAD_DIM] * (1.0 / denom))
    out = jnp.concatenate(heads, axis=0).T
    o_ref[...] = out.reshape(GROUP, ATTN_Q, KV_WIDTH).astype(BF16)


def _attention(sink, q, k, v, kx, vx, batch, seq, ctx_len):
    nb = seq // BLOCK
    per = ATTN_Q // BLOCK
    steps = seq // ATTN_Q
    t = batch * seq

    def cur(b, n):
        return (b * steps + n, 0)

    def prev(b, n):
        return (b * nb + jnp.maximum(n * per - 1, 0), 0)

    def nxt(b, n):
        return (b * nb + jnp.minimum((n + 1) * per, nb - 1), 0)

    kv = lambda f: pl.BlockSpec((BLOCK, KV_WIDTH), f)
    kc = pl.BlockSpec((ATTN_Q, KV_WIDTH), cur)
    cx = pl.BlockSpec((ctx_len, KV_WIDTH), lambda b, n: (b, 0))
    qo = pl.BlockSpec((GROUP, ATTN_Q, KV_WIDTH), lambda b, n: (0, b * steps + n, 0))
    return pl.pallas_call(
        functools.partial(_attn_kernel, n_steps=steps),
        grid=(batch, steps),
        in_specs=[pl.BlockSpec(memory_space=pltpu.SMEM), qo,
                  kv(prev), kc, kv(nxt), kv(prev), kc, kv(nxt), cx, cx],
        out_specs=qo,
        out_shape=jax.ShapeDtypeStruct((GROUP, t, KV_WIDTH), BF16),
        compiler_params=_params(2),
        name="attn",
    )(sink, q, k, k, k, v, v, v, kx, vx)


def _conv_kernel(up_ref, uc_ref, un_ref, dw_ref, dwb_ref, lng_ref, lnb_ref, o_ref, xp_ref, ph_ref,
                 *, tiles_per_seq, tc, rows):
    i = pl.program_id(0)
    first = (i % tiles_per_seq) == 0
    last = (i % tiles_per_seq) == tiles_per_seq - 1
    xp_ref[0:HALO, :] = jnp.where(first, 0.0, up_ref[...])
    xp_ref[HALO:HALO + tc, :] = uc_ref[...]
    xp_ref[HALO + tc:2 * HALO + tc, :] = jnp.where(last, 0.0, un_ref[...])
    span = ph_ref.shape[1]
    for r in range(SUBLANES):
        ph_ref[r] = xp_ref[r:r + span, :]
    for c in range(tc // rows):
        acc = jnp.zeros((rows, CONV_CH), F32)
        for k in range(CONV_WIDTH):
            q, r = divmod(k + HALO - CONV_PAD, SUBLANES)
            r0 = c * rows + q * SUBLANES
            acc = acc + ph_ref[r, r0:r0 + rows, :] * dw_ref[k:k + 1, :]
        acc = acc + dwb_ref[...]
        mu = jnp.mean(acc, axis=-1, keepdims=True)
        xc = acc - mu
        var = jnp.mean(xc * xc, axis=-1, keepdims=True)
        y = xc * lax.rsqrt(var + NORM_EPS) * lng_ref[...] + lnb_ref[...]
        o_ref[c * rows:(c + 1) * rows, :] = (y * _sigmoid(y)).astype(BF16)


def _conv(u, dw, dwb, lng, lnb, seq, tc, rows=64):
    t, ch = u.shape
    per = tc // HALO
    nh = t // HALO
    return pl.pallas_call(
        functools.partial(_conv_kernel, tiles_per_seq=seq // tc, tc=tc, rows=rows),
        grid=(t // tc,),
        in_specs=[pl.BlockSpec((HALO, ch), lambda i: (jnp.maximum(i * per - 1, 0), 0)),
                  pl.BlockSpec((tc, ch), lambda i: (i, 0)),
                  pl.BlockSpec((HALO, ch), lambda i: (jnp.minimum((i + 1) * per, nh - 1), 0)),
                  pl.BlockSpec((CONV_WIDTH, ch), lambda i: (0, 0)),
                  pl.BlockSpec((1, ch), lambda i: (0, 0)),
                  pl.BlockSpec((1, ch), lambda i: (0, 0)),
                  pl.BlockSpec((1, ch), lambda i: (0, 0))],
        out_specs=pl.BlockSpec((tc, ch), lambda i: (i, 0)),
        out_shape=jax.ShapeDtypeStruct((t, ch), BF16),
        scratch_shapes=[pltpu.VMEM((tc + 2 * HALO, ch), F32),
                        pltpu.VMEM((SUBLANES, tc + 2 * HALO - SUBLANES, ch), F32)],
        compiler_params=_params(1),
        name="conv",
    )(u, u, u, dw, dwb, lng, lnb)


def _merge_kernel(o_ref, hc_ref, gate_ref, x_ref, gt_ref, sh_ref, sc_ref, gpost_ref, gpre_ref,
                  woa_ref, woc_ref, wout_ref, x1_ref, h2t_ref):
    d = x_ref.shape[1]
    y_attn = jnp.dot(o_ref[0], woa_ref[0:KV_WIDTH, :], preferred_element_type=F32)
    for g in range(1, GROUP):
        y_attn = y_attn + jnp.dot(o_ref[g], woa_ref[g * KV_WIDTH:(g + 1) * KV_WIDTH, :],
                                  preferred_element_type=F32)
    y_conv = jnp.dot(hc_ref[...], woc_ref[...], preferred_element_type=F32)
    mix = (gate_ref[:, :d].astype(F32) * y_attn + gate_ref[:, d:].astype(F32) * y_conv).astype(BF16)
    mixed = jnp.dot(mix, wout_ref[...], preferred_element_type=F32)
    x1 = x_ref[...] + gt_ref[0] * _rms(mixed, gpost_ref[...])
    x1_ref[...] = x1
    h2 = _rms(x1, gpre_ref[...]) * (1.0 + sc_ref[0]) + sh_ref[0]
    h2t_ref[...] = h2.T.astype(BF16)


def _merge(o, hc, gates, x2, gt1, sh2, sc2, gpost, gpre, woa, woc, wout, seq, tm):
    t, d = x2.shape
    mod = pl.BlockSpec((1, 1, d), lambda i: ((i * tm) // seq, 0, 0))
    vec = pl.BlockSpec((1, d), lambda i: (0, 0))
    full = lambda a: pl.BlockSpec(a.shape, lambda i: (0, 0))
    return pl.pallas_call(
        _merge_kernel,
        grid=(t // tm,),
        in_specs=[pl.BlockSpec((GROUP, tm, KV_WIDTH), lambda i: (0, i, 0)),
                  pl.BlockSpec((tm, CONV_CH), lambda i: (i, 0)),
                  pl.BlockSpec((tm, 2 * d), lambda i: (i, 0)),
                  pl.BlockSpec((tm, d), lambda i: (i, 0)),
                  mod, mod, mod, vec, vec, full(woa), full(woc), full(wout)],
        out_specs=[pl.BlockSpec((tm, d), lambda i: (i, 0)),
                   pl.BlockSpec((d, tm), lambda i: (0, i))],
        out_shape=[jax.ShapeDtypeStruct((t, d), F32),
                   jax.ShapeDtypeStruct((d, t), BF16)],
        compiler_params=_params(1),
        name="merge",
    )(o, hc, gates, x2, gt1, sh2, sc2, gpost, gpre, woa, woc, wout)


def _top16(s):
    rem = s
    rank = jnp.full(s.shape, float(PEER_TOPK), F32)
    vals = []
    for j in range(PEER_TOPK):
        m = jnp.max(rem, axis=0, keepdims=True)
        eq = rem == m
        rank = jnp.where(eq, float(j), rank)
        rem = jnp.where(eq, NEG_BIG, rem)
        vals.append(m)
    return jnp.concatenate(vals, axis=0), rank


_SORT16 = ((0, 1), (2, 3), (0, 2), (1, 3), (1, 2), (4, 5), (6, 7), (4, 6), (5, 7), (5, 6), (0, 4), (2, 6), (2, 4),
           (1, 5), (3, 7), (3, 5), (1, 2), (3, 4), (5, 6), (8, 9), (10, 11), (8, 10), (9, 11), (9, 10), (12, 13),
           (14, 15), (12, 14), (13, 15), (13, 14), (8, 12), (10, 14), (10, 12), (9, 13), (11, 15), (11, 13), (9, 10),
           (11, 12), (13, 14), (0, 8), (4, 12), (4, 8), (2, 10), (6, 14), (6, 10), (2, 4), (6, 8), (10, 12), (1, 9),
           (5, 13), (5, 9), (3, 11), (7, 15), (7, 11), (3, 5), (7, 9), (11, 13), (1, 2), (3, 4), (5, 6), (7, 8),
           (9, 10), (11, 12), (13, 14))


def _top16_values(s):
    cols = [s[k * SUBLANES:(k + 1) * SUBLANES] for k in range(N_KEYS // SUBLANES)]
    for i, j in _SORT16:
        cols[i], cols[j] = jnp.maximum(cols[i], cols[j]), jnp.minimum(cols[i], cols[j])
    vals = []
    for j in range(PEER_TOPK):
        m = jnp.max(cols[0], axis=0, keepdims=True)
        vals.append(m)
        hit = cols[0] == m
        for k in range(PEER_TOPK - 1 - j):
            cols[k] = jnp.where(hit, cols[k + 1], cols[k])
    return jnp.concatenate(vals, axis=0)


def _pair_threshold(v1, v2):
    row = lax.broadcasted_iota(jnp.int32, v1.shape, 0)
    cols = [jnp.where(row < PEER_TOPK // (d + 1), v1 + v2[d:d + 1], NEG_BIG) for d in range(PEER_TOPK)]
    m = None
    for j in range(PEER_TOPK):
        m = jnp.max(cols[0], axis=0, keepdims=True)
        hit = cols[0] == m
        for k in range(PEER_TOPK - 1 - j):
            cols[k] = jnp.where(hit, cols[k + 1], cols[k])
    return m


def _retrieve(s1, s2):
    v1 = _top16_values(s1)
    v2, rank2 = _top16(s2)
    thr = _pair_threshold(v1, v2)
    ev1 = jnp.exp(v1 - v1[0:1])
    ev2 = jnp.exp(v2 - v2[0:1])
    z = jnp.zeros_like(thr)
    n = jnp.zeros(s1.shape, F32)
    for i in range(PEER_TOPK):
        sel = (v1[i:i + 1] + v2) >= thr
        cnt = jnp.sum(jnp.where(sel, 1.0, 0.0), axis=0, keepdims=True)
        z = z + ev1[i:i + 1] * jnp.sum(jnp.where(sel, ev2, 0.0), axis=0, keepdims=True)
        n = jnp.where(s1 == v1[i:i + 1], cnt, n)
    e1 = jnp.where(s1 >= v1[PEER_TOPK - 1:], jnp.exp(s1 - v1[0:1]) * (0.5 / z), 0.0)
    return n, e1, rank2.astype(BF16), jnp.exp(s2 - v2[0:1]).astype(BF16)


def _score_kernel(h2t_ref, wqt_ref, keys_ref, n_ref, e1_ref, r2_ref, e2_ref):
    qpt = jnp.dot(wqt_ref[...], h2t_ref[...], preferred_element_type=F32)
    for h in range(PEER_HEADS):
        q1 = qpt[h * PEER_QDIM:h * PEER_QDIM + PEER_HALF].astype(BF16)
        q2 = qpt[h * PEER_QDIM + PEER_HALF:(h + 1) * PEER_QDIM].astype(BF16)
        s1 = jnp.dot(keys_ref[h, 0], q1, preferred_element_type=F32)
        s2 = jnp.dot(keys_ref[h, 1], q2, preferred_element_type=F32)
        for l0 in range(0, s1.shape[1], LANES):
            ls = slice(l0, l0 + LANES)
            n_ref[h, :, ls], e1_ref[h, :, ls], r2_ref[h, :, ls], e2_ref[h, :, ls] = _retrieve(s1[:, ls], s2[:, ls])


def _score(h2t, wqt, keys, tt):
    d, t = h2t.shape
    blk = pl.BlockSpec((PEER_HEADS, N_KEYS, tt), lambda j: (0, 0, j))
    shp = lambda dt: jax.ShapeDtypeStruct((PEER_HEADS, N_KEYS, t), dt)
    return pl.pallas_call(
        _score_kernel,
        grid=(t // tt,),
        in_specs=[pl.BlockSpec((d, tt), lambda j: (0, j)),
                  pl.BlockSpec(wqt.shape, lambda j: (0, 0)),
                  pl.BlockSpec(keys.shape, lambda j: (0, 0, 0, 0))],
        out_specs=[blk, blk, blk, blk],
        out_shape=[shp(F32), shp(F32), shp(BF16), shp(BF16)],
        compiler_params=_params(1),
        name="peer_score",
    )(h2t, wqt, keys)


def _expert_kernel(h2t_ref, u_ref, vt_ref, n_ref, e1_ref, r2_ref, e2_ref, x1_ref, gt_ref, gpost_ref,
                   o_ref, acc_ref, at_ref, ga_ref, *, eb, ec):
    i = pl.program_id(1)

    @pl.when(i == 0)
    def _():
        acc_ref[...] = jnp.zeros_like(acc_ref)

    tt = h2t_ref.shape[1]
    per = ec // N_KEYS
    n_chunks = eb // ec
    pk = N_KEYS // BF16_ROWS

    def up(c, ls):
        at = jnp.dot(u_ref[c * ec:(c + 1) * ec, :], h2t_ref[:, ls], preferred_element_type=F32)
        at_ref[c % 2, :, ls] = at

    def down(c, ls):
        p = c // DOWN_CHUNKS
        acc_ref[:, ls] += jnp.dot(vt_ref[:, p * DOWN_CHUNKS * ec:(p + 1) * DOWN_CHUNKS * ec],
                                  ga_ref[p % 2, :, ls],
                                  preferred_element_type=F32)

    def gate(c, aa, ls):
        al = c * per + aa
        rows = slice(aa * N_KEYS, (aa + 1) * N_KEYS)
        ga_rows = slice(((c % DOWN_CHUNKS) * per + aa) * N_KEYS, ((c % DOWN_CHUNKS) * per + aa + 1) * N_KEYS)
        g = None
        for h in range(PEER_HEADS):
            nrow = jnp.broadcast_to(n_ref[h, al:al + 1, ls], (BF16_ROWS, GATE_LANES)).astype(BF16)
            erow = jnp.broadcast_to(e1_ref[h, al:al + 1, ls], (BF16_ROWS, GATE_LANES)).astype(BF16)
            term = jnp.where(r2_ref[h, :, :, ls] < nrow[None], e2_ref[h, :, :, ls] * erow[None], 0.0)
            g = term if g is None else g + term
        at = at_ref[c % 2, rows, ls]
        act = (at * (1.0 + lax.erf(at * SQRT_HALF))).astype(BF16)
        ga_ref[(c // DOWN_CHUNKS) % 2, ga_rows, ls] = g.reshape(N_KEYS, GATE_LANES) * act

    slabs = [slice(l0, l0 + GATE_LANES) for l0 in range(0, tt, GATE_LANES)]
    for ls in slabs:
        up(0, ls)
    for c in range(n_chunks):
        for ls in slabs:
            gate(c, 0, ls)
            if c + 1 < n_chunks:
                up(c + 1, ls)
            for aa in range(1, per):
                gate(c, aa, ls)
            if c % DOWN_CHUNKS == DOWN_CHUNKS - 1:
                down(c, ls)

    @pl.when(i == pl.num_programs(1) - 1)
    def _():
        y = acc_ref[...].T
        o_ref[...] = x1_ref[...] + gt_ref[0] * _rms(y, gpost_ref[...])


def _experts(h2t, u, vt, n, e1, r2, e2, x1, gt2, gpost, seq, tt, eb, ec=256):
    d, t = h2t.shape
    ne = u.shape[0]
    a = eb // N_KEYS
    pk = N_KEYS // BF16_ROWS
    sub = pl.BlockSpec((PEER_HEADS, a, tt), lambda j, i: (0, i, j))
    allk = pl.BlockSpec((PEER_HEADS, pk, BF16_ROWS, tt), lambda j, i: (0, 0, 0, j))
    tiled = lambda v: v.reshape(PEER_HEADS, pk, BF16_ROWS, t)
    return pl.pallas_call(
        functools.partial(_expert_kernel, eb=eb, ec=ec),
        grid=(t // tt, ne // eb),
        in_specs=[pl.BlockSpec((d, tt), lambda j, i: (0, j)),
                  pl.BlockSpec((eb, d), lambda j, i: (i, 0)),
                  pl.BlockSpec((d, eb), lambda j, i: (0, i)),
                  sub, sub, allk, allk,
                  pl.BlockSpec((tt, d), lambda j, i: (j, 0)),
                  pl.BlockSpec((1, 1, d), lambda j, i: ((j * tt) // seq, 0, 0)),
                  pl.BlockSpec((1, d), lambda j, i: (0, 0))],
        out_specs=pl.BlockSpec((tt, d), lambda j, i: (j, 0)),
        out_shape=jax.ShapeDtypeStruct((t, d), F32),
        scratch_shapes=[pltpu.VMEM((d, tt), F32),
                        pltpu.VMEM((2, ec, tt), F32),
                        pltpu.VMEM((2, DOWN_CHUNKS * ec, tt), BF16)],
        compiler_params=_params(2),
        name="peer_experts",
    )(h2t, u, vt, n, e1, tiled(r2), tiled(e2), x1, gt2, gpost)


def _rope_table(seq):
    t = jnp.arange(seq, dtype=jnp.int32)
    row = (t // GRID_W).astype(F32)
    col = (t % GRID_W).astype(F32)
    freqs = ROPE_BASE ** (-jnp.arange(0, ROPE_AXIS_DIM, 2, dtype=F32) / ROPE_AXIS_DIM)
    ang_r = row[:, None] * freqs[None, :]
    ang_c = col[:, None] * freqs[None, :]
    zero = jnp.zeros_like(ang_r)
    cos = jnp.concatenate([jnp.cos(ang_r), jnp.cos(ang_r), jnp.cos(ang_c), jnp.cos(ang_c)], axis=1)
    s_lo = jnp.concatenate([-jnp.sin(ang_r), zero, -jnp.sin(ang_c), zero], axis=1)
    s_hi = jnp.concatenate([zero, jnp.sin(ang_r), zero, jnp.sin(ang_c)], axis=1)
    two = lambda a: jnp.concatenate([a, a], axis=1)
    return jnp.concatenate([two(cos), two(s_lo), two(s_hi)], axis=1)


def _layer(x2, ctx2, cond, rope_tab, batch, seq, ctx_len,
           w_ada, b_ada, g_pre_mix, g_post_mix, g_pre_ffn, g_post_ffn, w_in, attn_sink, w_o_attn,
           conv_dw, conv_dw_b, conv_ln_g, conv_ln_b, w_o_conv, w_out, peer_wq, peer_keys, peer_u, peer_v):
    d = D_MODEL
    t = batch * seq
    tm = min(512, seq)
    row = lambda v: v.reshape(1, -1)

    mod = _ada(cond, w_ada, b_ada)
    per_batch = [mod[:batch, j * d:(j + 1) * d].reshape(batch, 1, d) for j in range(6)]
    sh1, sc1, gt1, sh2, sc2, gt2 = per_batch
    csh1 = mod[batch:batch + 1, 0:d]
    csc1 = mod[batch:batch + 1, d:2 * d]

    wq = w_in[:, :K_OFF].reshape(d, N_KV_HEADS, GROUP, HEAD_DIM).transpose(0, 2, 1, 3).reshape(d, ATTN_WIDTH)
    w_in_b = jnp.concatenate([wq, w_in[:, K_OFF:]], axis=1).astype(BF16)
    woa = w_o_attn.reshape(N_KV_HEADS, GROUP, HEAD_DIM, d).transpose(1, 0, 2, 3).reshape(ATTN_WIDTH, d).astype(BF16)

    q, k, v, u, gates = _in_proj(x2, sh1, sc1, row(g_pre_mix), rope_tab, w_in_b, seq, tm)
    kx, vx = _ctx_proj(ctx2, csh1, csc1, row(g_pre_mix), w_in_b[:, K_OFF:CONV_OFF], min(512, ctx2.shape[0]))
    o = _attention(attn_sink, q, k, v, kx, vx, batch, seq, ctx_len)
    hc = _conv(u, conv_dw, row(conv_dw_b), row(conv_ln_g), row(conv_ln_b), seq, min(256, seq))
    x1, h2t = _merge(o, hc, gates, x2, gt1, sh2, sc2, row(g_post_mix), row(g_pre_ffn),
                     woa, w_o_conv.astype(BF16), w_out.astype(BF16), seq, tm)
    n, e1, r2, e2 = _score(h2t, peer_wq.T.astype(BF16), peer_keys.astype(BF16), min(256, seq))
    return _experts(h2t, peer_u.astype(BF16), peer_v.T.astype(BF16), n, e1, r2, e2, x1, gt2,
                    row(g_post_ffn), seq, min(512, seq), 2048)


def kernel(x, c, ctx, c_ctx, w_ada, b_ada, g_pre_mix, g_post_mix, g_pre_ffn, g_post_ffn, w_in, attn_sink,
           w_o_attn, conv_dw, conv_dw_b, conv_ln_g, conv_ln_b, w_o_conv, w_out, peer_wq, peer_keys, peer_u,
           peer_v):
    batch, seq, d = x.shape
    ctx_len = ctx.shape[1]
    depth = w_ada.shape[0]
    assert depth == 1, "context-stream update between layers is not implemented"
    assert d == D_MODEL and seq % ATTN_Q == 0 and seq % GRID_W == 0 and batch < 16
    cond = jnp.zeros((16, d), F32).at[:batch].set(c).at[batch].set(c_ctx)
    rope_tab = _rope_table(seq)
    out = _layer(x.reshape(batch * seq, d), ctx.reshape(batch * ctx_len, d), cond, rope_tab, batch, seq, ctx_len,
                 w_ada[0], b_ada[0], g_pre_mix[0], g_post_mix[0], g_pre_ffn[0], g_post_ffn[0], w_in[0],
                 attn_sink[0], w_o_attn[0], conv_dw[0], conv_dw_b[0], conv_ln_g[0], conv_ln_b[0], w_o_conv[0],
                 w_out[0], peer_wq[0], peer_keys[0], peer_u[0], peer_v[0])
    return out.reshape(batch, seq, d)
```

```python
import functools

import jax
import jax.numpy as jnp
from jax import lax
from jax.experimental import pallas as pl
from jax.experimental.pallas import tpu as pltpu

F32 = jnp.float32
BF16 = jnp.bfloat16

D_MODEL = 1024
GRID_W = 64
N_HEADS = 16
N_KV_HEADS = 4
GROUP = N_HEADS // N_KV_HEADS
HEAD_DIM = 64
ATTN_WIDTH = N_HEADS * HEAD_DIM
KV_WIDTH = N_KV_HEADS * HEAD_DIM
WINDOW = 128
BLOCK = 128
ROPE_AXIS_DIM = HEAD_DIM // 2
ROPE_BASE = 10000.0
CONV_CH = D_MODEL // 2
CONV_WIDTH = 31
CONV_PAD = CONV_WIDTH // 2
K_OFF = ATTN_WIDTH
V_OFF = K_OFF + KV_WIDTH
CONV_OFF = V_OFF + KV_WIDTH
GATE_OFF = CONV_OFF + 2 * CONV_CH
IN_WIDTH = GATE_OFF + 2 * D_MODEL
PEER_HEADS = 8
N_KEYS = 128
N_EXPERTS = N_KEYS * N_KEYS
PEER_QDIM = 256
PEER_HALF = PEER_QDIM // 2
PEER_TOPK = 16
NORM_EPS = 1e-6
NEG_INF = -1e30
NEG_BIG = -3.0e38
SQRT_HALF = 0.7071067811865476
Q_SCALE = HEAD_DIM ** -0.5
LOG2E = 1.4426950408889634

LANES = 128
ATTN_Q = 2 * BLOCK
SUBLANES = 8
BF16_ROWS = 16
HALO = 16
GATE_LANES = 256
DOWN_CHUNKS = 2
VMEM_LIMIT = 56 * 1024 * 1024

_NT = (((1,), (1,)), ((), ()))
_TN = (((0,), (0,)), ((), ()))


def _params(n_axes, vmem=VMEM_LIMIT, flags=None):
    return pltpu.CompilerParams(dimension_semantics=("arbitrary",) * n_axes, vmem_limit_bytes=vmem, flags=flags)


def _rms(x, g):
    return x * lax.rsqrt(jnp.mean(x * x, axis=-1, keepdims=True) + NORM_EPS) * g


def _sigmoid(x):
    return jax.nn.sigmoid(x)


def _ada_kernel(c_ref, w_ref, b_ref, o_ref):
    c = c_ref[...]
    s = (c * _sigmoid(c)).astype(BF16)
    o_ref[...] = jnp.dot(s, w_ref[...].astype(BF16), preferred_element_type=F32) + b_ref[...]


def _ada(cond, w_ada, b_ada):
    rows, d = cond.shape
    n = w_ada.shape[1]
    bn = 1536
    return pl.pallas_call(
        _ada_kernel,
        grid=(n // bn,),
        in_specs=[pl.BlockSpec((rows, d), lambda j: (0, 0)),
                  pl.BlockSpec((d, bn), lambda j: (0, j)),
                  pl.BlockSpec((1, bn), lambda j: (0, j))],
        out_specs=pl.BlockSpec((rows, bn), lambda j: (0, j)),
        out_shape=jax.ShapeDtypeStruct((rows, n), F32),
        compiler_params=_params(1),
        name="ada",
    )(cond, w_ada, b_ada.reshape(1, n))


def _in_kernel(x_ref, sh_ref, sc_ref, g_ref, tab_ref, wq_ref, wr_ref, q_ref, k_ref, v_ref, u_ref, gate_ref):
    h = (_rms(x_ref[...], g_ref[...]) * (1.0 + sc_ref[0]) + sh_ref[0]).astype(BF16)
    cos = tab_ref[:, 0:LANES]
    sin_lo = tab_ref[:, LANES:2 * LANES]
    sin_hi = tab_ref[:, 2 * LANES:3 * LANES]

    def proj(lo, n):
        if lo < K_OFF:
            return jnp.dot(h, wq_ref[:, lo:lo + n], preferred_element_type=F32)
        return jnp.dot(h, wr_ref[:, lo - K_OFF:lo - K_OFF + n], preferred_element_type=F32)

    def rope(p):
        return (p * cos + pltpu.roll(p, LANES - ROPE_AXIS_DIM // 2, 1) * sin_lo
                + pltpu.roll(p, ROPE_AXIS_DIM // 2, 1) * sin_hi)

    def rope2(p):
        return jnp.concatenate([rope(p[:, :LANES]), rope(p[:, LANES:])], axis=1)

    for g in range(GROUP):
        q_ref[g] = (rope2(proj(g * KV_WIDTH, KV_WIDTH)) * (Q_SCALE * LOG2E)).astype(BF16)
    k_ref[...] = rope2(proj(K_OFF, KV_WIDTH)).astype(BF16)
    v_ref[...] = proj(V_OFF, KV_WIDTH).astype(BF16)
    u_ref[...] = proj(CONV_OFF, CONV_CH) * _sigmoid(proj(CONV_OFF + CONV_CH, CONV_CH))
    for j in range(4):
        gate_ref[:, j * 512:(j + 1) * 512] = _sigmoid(proj(GATE_OFF + j * 512, 512)).astype(BF16)


def _in_proj(x2, sh, sc, g, tab, wq, wr, seq, tm):
    t, d = x2.shape
    return pl.pallas_call(
        _in_kernel,
        grid=(t // tm,),
        in_specs=[pl.BlockSpec((tm, d), lambda i: (i, 0)),
                  pl.BlockSpec((1, 1, d), lambda i: ((i * tm) // seq, 0, 0)),
                  pl.BlockSpec((1, 1, d), lambda i: ((i * tm) // seq, 0, 0)),
                  pl.BlockSpec((1, d), lambda i: (0, 0)),
                  pl.BlockSpec((tm, 3 * LANES), lambda i: (i % (seq // tm), 0)),
                  pl.BlockSpec((d, K_OFF), lambda i: (0, 0)),
                  pl.BlockSpec((d, IN_WIDTH - K_OFF), lambda i: (0, 0))],
        out_specs=[pl.BlockSpec((GROUP, tm, KV_WIDTH), lambda i: (0, i, 0)),
                   pl.BlockSpec((tm, KV_WIDTH), lambda i: (i, 0)),
                   pl.BlockSpec((tm, KV_WIDTH), lambda i: (i, 0)),
                   pl.BlockSpec((tm, CONV_CH), lambda i: (i, 0)),
                   pl.BlockSpec((tm, 2 * d), lambda i: (i, 0))],
        out_shape=[jax.ShapeDtypeStruct((GROUP, t, KV_WIDTH), BF16),
                   jax.ShapeDtypeStruct((t, KV_WIDTH), BF16),
                   jax.ShapeDtypeStruct((t, KV_WIDTH), BF16),
                   jax.ShapeDtypeStruct((t, CONV_CH), F32),
                   jax.ShapeDtypeStruct((t, 2 * d), BF16)],
        compiler_params=_params(1),
        name="in_proj",
    )(x2, sh, sc, g, tab, wq, wr)


def _ctx_kernel(x_ref, sh_ref, sc_ref, g_ref, w_ref, k_ref, v_ref):
    h = (_rms(x_ref[...], g_ref[...]) * (1.0 + sc_ref[...]) + sh_ref[...]).astype(BF16)
    k_ref[...] = jnp.dot(h, w_ref[:, :KV_WIDTH], preferred_element_type=F32).astype(BF16)
    v_ref[...] = jnp.dot(h, w_ref[:, KV_WIDTH:], preferred_element_type=F32).astype(BF16)


def _ctx_proj(c2, sh, sc, g, w, tm):
    t, d = c2.shape
    return pl.pallas_call(
        _ctx_kernel,
        grid=(t // tm,),
        in_specs=[pl.BlockSpec((tm, d), lambda i: (i, 0)),
                  pl.BlockSpec((1, d), lambda i: (0, 0)),
                  pl.BlockSpec((1, d), lambda i: (0, 0)),
                  pl.BlockSpec((1, d), lambda i: (0, 0)),
                  pl.BlockSpec((d, 2 * KV_WIDTH), lambda i: (0, 0))],
        out_specs=[pl.BlockSpec((tm, KV_WIDTH), lambda i: (i, 0)),
                   pl.BlockSpec((tm, KV_WIDTH), lambda i: (i, 0))],
        out_shape=[jax.ShapeDtypeStruct((t, KV_WIDTH), BF16),
                   jax.ShapeDtypeStruct((t, KV_WIDTH), BF16)],
        compiler_params=_params(1),
        name="ctx_proj",
    )(c2, sh, sc, g, w)


def _attn_kernel(sink_ref, q_ref, kp_ref, kc_ref, kn_ref, vp_ref, vc_ref, vn_ref, kx_ref, vx_ref,
                 o_ref, *, n_steps):
    n = pl.program_id(1)
    qs = q_ref[...].reshape(GROUP * ATTN_Q, KV_WIDTH)
    kw = jnp.concatenate([kp_ref[...], kc_ref[...], kn_ref[...], kx_ref[...]], axis=0)
    vw = jnp.concatenate([vp_ref[...], vc_ref[...], vn_ref[...], vx_ref[...]], axis=0)
    nk = kw.shape[0]
    nwin = ATTN_Q + 2 * BLOCK
    kj = lax.broadcasted_iota(jnp.int32, (nwin, ATTN_Q), 0)
    qi = lax.broadcasted_iota(jnp.int32, (nwin, ATTN_Q), 1)
    ok = (kj >= qi) & (kj <= qi + 2 * WINDOW)
    ok = ok & ((kj >= BLOCK) | (n > 0)) & ((kj < nwin - BLOCK) | (n < n_steps - 1))
    bias = jnp.concatenate([jnp.where(ok, 0.0, NEG_INF).astype(F32)] * GROUP, axis=1)
    lane = lax.broadcasted_iota(jnp.int32, (nk, KV_WIDTH), 1)
    scores = []
    for h in range(N_KV_HEADS):
        km = jnp.where(lane // HEAD_DIM == h, kw, jnp.zeros_like(kw))
        st = lax.dot_general(km, qs, _NT, preferred_element_type=F32)
        scores.append(jnp.concatenate([st[:nwin] + bias, st[nwin:]], axis=0))
    vt = vw.astype(F32).T.astype(BF16)
    ones = jnp.ones((BF16_ROWS, nk), BF16)
    heads = []
    for h in range(N_KV_HEADS):
        st = scores[h]
        sink = jnp.concatenate(
            [jnp.full((1, ATTN_Q), sink_ref[h * GROUP + g] * LOG2E, F32) for g in range(GROUP)], axis=1)
        m = jnp.maximum(jnp.max(st, axis=0, keepdims=True), sink)
        pt = jnp.exp2(st - m).astype(BF16)
        lhs = jnp.concatenate([vt[h * HEAD_DIM:(h + 1) * HEAD_DIM], ones], axis=0)
        rest = jnp.dot(lhs, pt, preferred_element_type=F32)
        denom = rest[HEAD_DIM:HEAD_DIM + 1] + jnp.exp2(sink - m)
        heads.append(rest[:HEAD_DIM] * (1.0 / denom))
    out = jnp.concatenate(heads, axis=0).T
    o_ref[...] = out.reshape(GROUP, ATTN_Q, KV_WIDTH).astype(BF16)


def _attention(sink, q, k, v, kx, vx, batch, seq, ctx_len):
    nb = seq // BLOCK
    per = ATTN_Q // BLOCK
    steps = seq // ATTN_Q
    t = batch * seq

    def cur(b, n):
        return (b * steps + n, 0)

    def prev(b, n):
        return (b * nb + jnp.maximum(n * per - 1, 0), 0)

    def nxt(b, n):
        return (b * nb + jnp.minimum((n + 1) * per, nb - 1), 0)

    kv = lambda f: pl.BlockSpec((BLOCK, KV_WIDTH), f)
    kc = pl.BlockSpec((ATTN_Q, KV_WIDTH), cur)
    cx = pl.BlockSpec((ctx_len, KV_WIDTH), lambda b, n: (b, 0))
    qo = pl.BlockSpec((GROUP, ATTN_Q, KV_WIDTH), lambda b, n: (0, b * steps + n, 0))
    return pl.pallas_call(
        functools.partial(_attn_kernel, n_steps=steps),
        grid=(batch, steps),
        in_specs=[pl.BlockSpec(memory_space=pltpu.SMEM), qo,
                  kv(prev), kc, kv(nxt), kv(prev), kc, kv(nxt), cx, cx],
        out_specs=qo,
        out_shape=jax.ShapeDtypeStruct((GROUP, t, KV_WIDTH), BF16),
        compiler_params=_params(2),
        name="attn",
    )(sink, q, k, k, k, v, v, v, kx, vx)


def _conv_kernel(up_ref, uc_ref, un_ref, dw_ref, dwb_ref, lng_ref, lnb_ref, o_ref, xp_ref, ph_ref,
                 *, tiles_per_seq, tc, rows):
    i = pl.program_id(0)
    first = (i % tiles_per_seq) == 0
    last = (i % tiles_per_seq) == tiles_per_seq - 1
    xp_ref[0:HALO, :] = jnp.where(first, 0.0, up_ref[...])
    xp_ref[HALO:HALO + tc, :] = uc_ref[...]
    xp_ref[HALO + tc:2 * HALO + tc, :] = jnp.where(last, 0.0, un_ref[...])
    span = ph_ref.shape[1]
    for r in range(SUBLANES):
        ph_ref[r] = xp_ref[r:r + span, :]
    for c in range(tc // rows):
        acc = jnp.zeros((rows, CONV_CH), F32)
        for k in range(CONV_WIDTH):
            q, r = divmod(k + HALO - CONV_PAD, SUBLANES)
            r0 = c * rows + q * SUBLANES
            acc = acc + ph_ref[r, r0:r0 + rows, :] * dw_ref[k:k + 1, :]
        acc = acc + dwb_ref[...]
        mu = jnp.mean(acc, axis=-1, keepdims=True)
        xc = acc - mu
        var = jnp.mean(xc * xc, axis=-1, keepdims=True)
        y = xc * lax.rsqrt(var + NORM_EPS) * lng_ref[...] + lnb_ref[...]
        o_ref[c * rows:(c + 1) * rows, :] = (y * _sigmoid(y)).astype(BF16)


def _conv(u, dw, dwb, lng, lnb, seq, tc, rows=64):
    t, ch = u.shape
    per = tc // HALO
    nh = t // HALO
    return pl.pallas_call(
        functools.partial(_conv_kernel, tiles_per_seq=seq // tc, tc=tc, rows=rows),
        grid=(t // tc,),
        in_specs=[pl.BlockSpec((HALO, ch), lambda i: (jnp.maximum(i * per - 1, 0), 0)),
                  pl.BlockSpec((tc, ch), lambda i: (i, 0)),
                  pl.BlockSpec((HALO, ch), lambda i: (jnp.minimum((i + 1) * per, nh - 1), 0)),
                  pl.BlockSpec((CONV_WIDTH, ch), lambda i: (0, 0)),
                  pl.BlockSpec((1, ch), lambda i: (0, 0)),
                  pl.BlockSpec((1, ch), lambda i: (0, 0)),
                  pl.BlockSpec((1, ch), lambda i: (0, 0))],
        out_specs=pl.BlockSpec((tc, ch), lambda i: (i, 0)),
        out_shape=jax.ShapeDtypeStruct((t, ch), BF16),
        scratch_shapes=[pltpu.VMEM((tc + 2 * HALO, ch), F32),
                        pltpu.VMEM((SUBLANES, tc + 2 * HALO - SUBLANES, ch), F32)],
        compiler_params=_params(1),
        name="conv",
    )(u, u, u, dw, dwb, lng, lnb)


def _merge_kernel(o_ref, hc_ref, gate_ref, x_ref, gt_ref, sh_ref, sc_ref, gpost_ref, gpre_ref,
                  woa_ref, woc_ref, wout_ref, x1_ref, h2t_ref):
    d = x_ref.shape[1]
    y_attn = jnp.dot(o_ref[0], woa_ref[0:KV_WIDTH, :], preferred_element_type=F32)
    for g in range(1, GROUP):
        y_attn = y_attn + jnp.dot(o_ref[g], woa_ref[g * KV_WIDTH:(g + 1) * KV_WIDTH, :],
                                  preferred_element_type=F32)
    y_conv = jnp.dot(hc_ref[...], woc_ref[...], preferred_element_type=F32)
    mix = (gate_ref[:, :d].astype(F32) * y_attn + gate_ref[:, d:].astype(F32) * y_conv).astype(BF16)
    mixed = jnp.dot(mix, wout_ref[...], preferred_element_type=F32)
    x1 = x_ref[...] + gt_ref[0] * _rms(mixed, gpost_ref[...])
    x1_ref[...] = x1
    h2 = _rms(x1, gpre_ref[...]) * (1.0 + sc_ref[0]) + sh_ref[0]
    h2t_ref[...] = h2.T.astype(BF16)


def _merge(o, hc, gates, x2, gt1, sh2, sc2, gpost, gpre, woa, woc, wout, seq, tm):
    t, d = x2.shape
    mod = pl.BlockSpec((1, 1, d), lambda i: ((i * tm) // seq, 0, 0))
    vec = pl.BlockSpec((1, d), lambda i: (0, 0))
    full = lambda a: pl.BlockSpec(a.shape, lambda i: (0, 0))
    return pl.pallas_call(
        _merge_kernel,
        grid=(t // tm,),
        in_specs=[pl.BlockSpec((GROUP, tm, KV_WIDTH), lambda i: (0, i, 0)),
                  pl.BlockSpec((tm, CONV_CH), lambda i: (i, 0)),
                  pl.BlockSpec((tm, 2 * d), lambda i: (i, 0)),
                  pl.BlockSpec((tm, d), lambda i: (i, 0)),
                  mod, mod, mod, vec, vec, full(woa), full(woc), full(wout)],
        out_specs=[pl.BlockSpec((tm, d), lambda i: (i, 0)),
                   pl.BlockSpec((d, tm), lambda i: (0, i))],
        out_shape=[jax.ShapeDtypeStruct((t, d), F32),
                   jax.ShapeDtypeStruct((d, t), BF16)],
        compiler_params=_params(1),
        name="merge",
    )(o, hc, gates, x2, gt1, sh2, sc2, gpost, gpre, woa, woc, wout)


def _top16(s):
    rem = s
    rank = jnp.full(s.shape, float(PEER_TOPK), F32)
    vals = []
    for j in range(PEER_TOPK):
        m = jnp.max(rem, axis=0, keepdims=True)
        eq = rem == m
        rank = jnp.where(eq, float(j), rank)
        rem = jnp.where(eq, NEG_BIG, rem)
        vals.append(m)
    return jnp.concatenate(vals, axis=0), rank


_SORT16 = ((0, 1), (2, 3), (0, 2), (1, 3), (1, 2), (4, 5), (6, 7), (4, 6), (5, 7), (5, 6), (0, 4), (2, 6), (2, 4),
           (1, 5), (3, 7), (3, 5), (1, 2), (3, 4), (5, 6), (8, 9), (10, 11), (8, 10), (9, 11), (9, 10), (12, 13),
           (14, 15), (12, 14), (13, 15), (13, 14), (8, 12), (10, 14), (10, 12), (9, 13), (11, 15), (11, 13), (9, 10),
           (11, 12), (13, 14), (0, 8), (4, 12), (4, 8), (2, 10), (6, 14), (6, 10), (2, 4), (6, 8), (10, 12), (1, 9),
           (5, 13), (5, 9), (3, 11), (7, 15), (7, 11), (3, 5), (7, 9), (11, 13), (1, 2), (3, 4), (5, 6), (7, 8),
           (9, 10), (11, 12), (13, 14))


def _top16_values(s):
    cols = [s[k * SUBLANES:(k + 1) * SUBLANES] for k in range(N_KEYS // SUBLANES)]
    for i, j in _SORT16:
        cols[i], cols[j] = jnp.maximum(cols[i], cols[j]), jnp.minimum(cols[i], cols[j])
    vals = []
    for j in range(PEER_TOPK):
        m = jnp.max(cols[0], axis=0, keepdims=True)
        vals.append(m)
        hit = cols[0] == m
        for k in range(PEER_TOPK - 1 - j):
            cols[k] = jnp.where(hit, cols[k + 1], cols[k])
    return jnp.concatenate(vals, axis=0)


def _pair_threshold(v1, v2):
    row = lax.broadcasted_iota(jnp.int32, v1.shape, 0)
    cols = [jnp.where(row < PEER_TOPK // (d + 1), v1 + v2[d:d + 1], NEG_BIG) for d in range(PEER_TOPK)]
    m = None
    for j in range(PEER_TOPK):
        m = jnp.max(cols[0], axis=0, keepdims=True)
        hit = cols[0] == m
        for k in range(PEER_TOPK - 1 - j):
            cols[k] = jnp.where(hit, cols[k + 1], cols[k])
    return m


def _retrieve(s1, s2):
    v1 = _top16_values(s1)
    v2, rank2 = _top16(s2)
    thr = _pair_threshold(v1, v2)
    ev1 = jnp.exp(v1 - v1[0:1])
    ev2 = jnp.exp(v2 - v2[0:1])
    z = jnp.zeros_like(thr)
    n = jnp.zeros(s1.shape, F32)
    for i in range(PEER_TOPK):
        sel = (v1[i:i + 1] + v2) >= thr
        cnt = jnp.sum(jnp.where(sel, 1.0, 0.0), axis=0, keepdims=True)
        z = z + ev1[i:i + 1] * jnp.sum(jnp.where(sel, ev2, 0.0), axis=0, keepdims=True)
        n = jnp.where(s1 == v1[i:i + 1], cnt, n)
    e1 = jnp.where(s1 >= v1[PEER_TOPK - 1:], jnp.exp(s1 - v1[0:1]) * (0.5 / z), 0.0)
    return n, e1, rank2.astype(BF16), jnp.exp(s2 - v2[0:1]).astype(BF16)


def _score_kernel(h2t_ref, wqt_ref, keys_ref, n_ref, e1_ref, r2_ref, e2_ref):
    qpt = jnp.dot(wqt_ref[...], h2t_ref[...], preferred_element_type=F32)
    for h in range(PEER_HEADS):
        q1 = qpt[h * PEER_QDIM:h * PEER_QDIM + PEER_HALF].astype(BF16)
        q2 = qpt[h * PEER_QDIM + PEER_HALF:(h + 1) * PEER_QDIM].astype(BF16)
        s1 = jnp.dot(keys_ref[h, 0], q1, preferred_element_type=F32)
        s2 = jnp.dot(keys_ref[h, 1], q2, preferred_element_type=F32)
        for l0 in range(0, s1.shape[1], LANES):
            ls = slice(l0, l0 + LANES)
            n_ref[h, :, ls], e1_ref[h, :, ls], r2_ref[h, :, ls], e2_ref[h, :, ls] = _retrieve(s1[:, ls], s2[:, ls])


def _score(h2t, wqt, keys, tt):
    d, t = h2t.shape
    blk = pl.BlockSpec((PEER_HEADS, N_KEYS, tt), lambda j: (0, 0, j))
    shp = lambda dt: jax.ShapeDtypeStruct((PEER_HEADS, N_KEYS, t), dt)
    return pl.pallas_call(
        _score_kernel,
        grid=(t // tt,),
        in_specs=[pl.BlockSpec((d, tt), lambda j: (0, j)),
                  pl.BlockSpec(wqt.shape, lambda j: (0, 0)),
                  pl.BlockSpec(keys.shape, lambda j: (0, 0, 0, 0))],
        out_specs=[blk, blk, blk, blk],
        out_shape=[shp(F32), shp(F32), shp(BF16), shp(BF16)],
        compiler_params=_params(1),
        name="peer_score",
    )(h2t, wqt, keys)


def _expert_kernel(h2t_ref, u_ref, vt_ref, n_ref, e1_ref, r2_ref, e2_ref, x1_ref, gt_ref, gpost_ref,
                   o_ref, acc_ref, at_ref, ga_ref, *, eb, ec):
    i = pl.program_id(1)

    @pl.when(i == 0)
    def _():
        acc_ref[...] = jnp.zeros_like(acc_ref)

    tt = h2t_ref.shape[1]
    per = ec // N_KEYS
    n_chunks = eb // ec
    pk = N_KEYS // BF16_ROWS

    def up(c, ls):
        at = jnp.dot(u_ref[c * ec:(c + 1) * ec, :], h2t_ref[:, ls], preferred_element_type=F32)
        at_ref[c % 2, :, ls] = at

    def down(c, ls):
        p = c // DOWN_CHUNKS
        acc_ref[:, ls] += jnp.dot(vt_ref[:, p * DOWN_CHUNKS * ec:(p + 1) * DOWN_CHUNKS * ec],
                                  ga_ref[p % 2, :, ls],
                                  preferred_element_type=F32)

    def gate(c, aa, ls):
        al = c * per + aa
        rows = slice(aa * N_KEYS, (aa + 1) * N_KEYS)
        ga_rows = slice(((c % DOWN_CHUNKS) * per + aa) * N_KEYS, ((c % DOWN_CHUNKS) * per + aa + 1) * N_KEYS)
        g = None
        for h in range(PEER_HEADS):
            nrow = jnp.broadcast_to(n_ref[h, al:al + 1, ls], (BF16_ROWS, GATE_LANES)).astype(BF16)
            erow = jnp.broadcast_to(e1_ref[h, al:al + 1, ls], (BF16_ROWS, GATE_LANES)).astype(BF16)
            term = jnp.where(r2_ref[h, :, :, ls] < nrow[None], e2_ref[h, :, :, ls] * erow[None], 0.0)
            g = term if g is None else g + term
        at = at_ref[c % 2, rows, ls]
        act = (at * (1.0 + lax.erf(at * SQRT_HALF))).astype(BF16)
        ga_ref[(c // DOWN_CHUNKS) % 2, ga_rows, ls] = g.reshape(N_KEYS, GATE_LANES) * act

    slabs = [slice(l0, l0 + GATE_LANES) for l0 in range(0, tt, GATE_LANES)]
    for ls in slabs:
        up(0, ls)
    for c in range(n_chunks):
        for ls in slabs:
            gate(c, 0, ls)
            if c + 1 < n_chunks:
                up(c + 1, ls)
            for aa in range(1, per):
                gate(c, aa, ls)
            if c % DOWN_CHUNKS == DOWN_CHUNKS - 1:
                down(c, ls)

    @pl.when(i == pl.num_programs(1) - 1)
    def _():
        y = acc_ref[...].T
        o_ref[...] = x1_ref[...] + gt_ref[0] * _rms(y, gpost_ref[...])


def _experts(h2t, u, vt, n, e1, r2, e2, x1, gt2, gpost, seq, tt, eb, ec=256):
    d, t = h2t.shape
    ne = u.shape[0]
    a = eb // N_KEYS
    pk = N_KEYS // BF16_ROWS
    sub = pl.BlockSpec((PEER_HEADS, a, tt), lambda j, i: (0, i, j))
    allk = pl.BlockSpec((PEER_HEADS, pk, BF16_ROWS, tt), lambda j, i: (0, 0, 0, j))
    tiled = lambda v: v.reshape(PEER_HEADS, pk, BF16_ROWS, t)
    return pl.pallas_call(
        functools.partial(_expert_kernel, eb=eb, ec=ec),
        grid=(t // tt, ne // eb),
        in_specs=[pl.BlockSpec((d, tt), lambda j, i: (0, j)),
                  pl.BlockSpec((eb, d), lambda j, i: (i, 0)),
                  pl.BlockSpec((d, eb), lambda j, i: (0, i)),
                  sub, sub, allk, allk,
                  pl.BlockSpec((tt, d), lambda j, i: (j, 0)),
                  pl.BlockSpec((1, 1, d), lambda j, i: ((j * tt) // seq, 0, 0)),
                  pl.BlockSpec((1, d), lambda j, i: (0, 0))],
        out_specs=pl.BlockSpec((tt, d), lambda j, i: (j, 0)),
        out_shape=jax.ShapeDtypeStruct((t, d), F32),
        scratch_shapes=[pltpu.VMEM((d, tt), F32),
                        pltpu.VMEM((2, ec, tt), F32),
                        pltpu.VMEM((2, DOWN_CHUNKS * ec, tt), BF16)],
        compiler_params=_params(2),
        name="peer_experts",
    )(h2t, u, vt, n, e1, tiled(r2), tiled(e2), x1, gt2, gpost)


def _rope_table(seq):
    t = jnp.arange(seq, dtype=jnp.int32)
    row = (t // GRID_W).astype(F32)
    col = (t % GRID_W).astype(F32)
    freqs = ROPE_BASE ** (-jnp.arange(0, ROPE_AXIS_DIM, 2, dtype=F32) / ROPE_AXIS_DIM)
    ang_r = row[:, None] * freqs[None, :]
    ang_c = col[:, None] * freqs[None, :]
    zero = jnp.zeros_like(ang_r)
    cos = jnp.concatenate([jnp.cos(ang_r), jnp.cos(ang_r), jnp.cos(ang_c), jnp.cos(ang_c)], axis=1)
    s_lo = jnp.concatenate([-jnp.sin(ang_r), zero, -jnp.sin(ang_c), zero], axis=1)
    s_hi = jnp.concatenate([zero, jnp.sin(ang_r), zero, jnp.sin(ang_c)], axis=1)
    two = lambda a: jnp.concatenate([a, a], axis=1)
    return jnp.concatenate([two(cos), two(s_lo), two(s_hi)], axis=1)


def _layer(x2, ctx2, cond, rope_tab, batch, seq, ctx_len,
           w_ada, b_ada, g_pre_mix, g_post_mix, g_pre_ffn, g_post_ffn, w_in, attn_sink, w_o_attn,
           conv_dw, conv_dw_b, conv_ln_g, conv_ln_b, w_o_conv, w_out, peer_wq, peer_keys, peer_u, peer_v):
    d = D_MODEL
    t = batch * seq
    tm = min(512, seq)
    row = lambda v: v.reshape(1, -1)

    mod = _ada(cond, w_ada, b_ada)
    per_batch = [mod[:batch, j * d:(j + 1) * d].reshape(batch, 1, d) for j in range(6)]
    sh1, sc1, gt1, sh2, sc2, gt2 = per_batch
    csh1 = mod[batch:batch + 1, 0:d]
    csc1 = mod[batch:batch + 1, d:2 * d]

    wq = w_in[:, :K_OFF].reshape(d, N_KV_HEADS, GROUP, HEAD_DIM).transpose(0, 2, 1, 3).reshape(d, ATTN_WIDTH)
    wq = wq.astype(BF16)
    wr = w_in[:, K_OFF:].astype(BF16)
    woa = w_o_attn.reshape(N_KV_HEADS, GROUP, HEAD_DIM, d).transpose(1, 0, 2, 3).reshape(ATTN_WIDTH, d).astype(BF16)

    q, k, v, u, gates = _in_proj(x2, sh1, sc1, row(g_pre_mix), rope_tab, wq, wr, seq, tm)
    kx, vx = _ctx_proj(ctx2, csh1, csc1, row(g_pre_mix), wr[:, :CONV_OFF - K_OFF], min(512, ctx2.shape[0]))
    o = _attention(attn_sink, q, k, v, kx, vx, batch, seq, ctx_len)
    hc = _conv(u, conv_dw, row(conv_dw_b), row(conv_ln_g), row(conv_ln_b), seq, min(256, seq))
    x1, h2t = _merge(o, hc, gates, x2, gt1, sh2, sc2, row(g_post_mix), row(g_pre_ffn),
                     woa, w_o_conv.astype(BF16), w_out.astype(BF16), seq, tm)
    n, e1, r2, e2 = _score(h2t, peer_wq.T.astype(BF16), peer_keys.astype(BF16), min(256, seq))
    return _experts(h2t, peer_u.astype(BF16), peer_v.T.astype(BF16), n, e1, r2, e2, x1, gt2,
                    row(g_post_ffn), seq, min(512, seq), 2048)


def kernel(x, c, ctx, c_ctx, w_ada, b_ada, g_pre_mix, g_post_mix, g_pre_ffn, g_post_ffn, w_in, attn_sink,
           w_o_attn, conv_dw, conv_dw_b, conv_ln_g, conv_ln_b, w_o_conv, w_out, peer_wq, peer_keys, peer_u,
           peer_v):
    batch, seq, d = x.shape
    ctx_len = ctx.shape[1]
    depth = w_ada.shape[0]
    assert depth == 1, "context-stream update between layers is not implemented"
    assert d == D_MODEL and seq % ATTN_Q == 0 and seq % GRID_W == 0 and batch < 16
    cond = jnp.zeros((16, d), F32).at[:batch].set(c).at[batch].set(c_ctx)
    rope_tab = _rope_table(seq)
    out = _layer(x.reshape(batch * seq, d), ctx.reshape(batch * ctx_len, d), cond, rope_tab, batch, seq, ctx_len,
                 w_ada[0], b_ada[0], g_pre_mix[0], g_post_mix[0], g_pre_ffn[0], g_post_ffn[0], w_in[0],
                 attn_sink[0], w_o_attn[0], conv_dw[0], conv_dw_b[0], conv_ln_g[0], conv_ln_b[0], w_o_conv[0],
                 w_out[0], peer_wq[0], peer_keys[0], peer_u[0], peer_v[0])
    return out.reshape(batch, seq, d)
```
